```python
import jax, jax.numpy as jnp
from jax import lax
import numpy as np

D_MODEL = 4096
BATCH = 4
SEQ = 4096
DEPTH = 4
DEC_BATCH = 8
DEC_SEQ = 2048
PAST_LEN = 128

GRID_W = 64
N_MIXERS = 2
N_NAT = (DEPTH + 1) // 2
N_GLA = DEPTH // 2
EPS = 1e-6
NAT_HEADS = 32
NAT_HEAD_DIM = D_MODEL // NAT_HEADS
WIN_H = 8
WIN_W = 16
GLA_HEADS = 4
GLA_DK = D_MODEL // 2
GLA_DV = D_MODEL
GLA_DK_HEAD = GLA_DK // GLA_HEADS
GLA_DV_HEAD = GLA_DV // GLA_HEADS
GLA_RANK = 16
GLA_GATE_NORM = 16.0
GLA_CHUNK = 64
GLA_IN = 2 * GLA_DK + 2 * GLA_DV + 2 * GLA_RANK
D_FF = ((8 * D_MODEL // 3 + 255) // 256) * 256

kernel_name = "hybrid_natten_gla_encoder"


def _rmsnorm(x, g):
    xf = x.astype(jnp.float32)
    y = xf * lax.rsqrt(jnp.mean(xf * xf, axis=-1, keepdims=True) + EPS)
    return (y * g.astype(jnp.float32)).astype(x.dtype)


def _swiglu(h, w_gu, w_down):
    gu = h @ w_gu
    g, u = jnp.split(gu, 2, axis=-1)
    return (jax.nn.silu(g) * u) @ w_down


def _natten(h, w_qkv, w_o, rpb):
    B, S, _ = h.shape
    rows = S // GRID_W
    kh = min(WIN_H, rows)
    qkv = h @ w_qkv
    q, k, v = jnp.split(qkv, 3, axis=-1)
    shp = (B, rows, GRID_W, NAT_HEADS, NAT_HEAD_DIM)
    q = q.reshape(shp) * (NAT_HEAD_DIM ** -0.5)
    k = k.reshape(shp)
    v = v.reshape(shp)
    row_start = jnp.clip(jnp.arange(rows) - kh // 2, 0, rows - kh)
    col_start = jnp.clip(jnp.arange(GRID_W) - WIN_W // 2, 0, GRID_W - WIN_W)
    col_idx = col_start[:, None] + jnp.arange(WIN_W)[None, :]
    dc = col_idx - jnp.arange(GRID_W)[:, None] + (WIN_W - 1)

    def one_row(r):
        rs = row_start[r]
        kb = lax.dynamic_slice_in_dim(k, rs, kh, axis=1)
        vb = lax.dynamic_slice_in_dim(v, rs, kh, axis=1)
        kg = kb[:, :, col_idx]
        vg = vb[:, :, col_idx]
        qr = lax.dynamic_index_in_dim(q, r, axis=1, keepdims=False)
        dr = rs + jnp.arange(kh) - r + (WIN_H - 1)
        bias = rpb[:, dr[:, None, None], dc[None, :, :]]
        s = jnp.einsum('bchd,bacwhd->bhcaw', qr, kg).astype(jnp.float32)
        s = s + jnp.transpose(bias, (0, 2, 1, 3))[None].astype(jnp.float32)
        p = jax.nn.softmax(s.reshape(B, NAT_HEADS, GRID_W, kh * WIN_W), axis=-1)
        p = p.reshape(B, NAT_HEADS, GRID_W, kh, WIN_W).astype(vg.dtype)
        return jnp.einsum('bhcaw,bacwhd->bchd', p, vg)

    out = lax.map(one_row, jnp.arange(rows))
    out = jnp.transpose(out, (1, 0, 2, 3, 4)).reshape(B, S, D_MODEL)
    return out @ w_o


def _gla_chunked(q, k, v, g):
    B, H, S, dk = q.shape
    dv = v.shape[-1]
    n = S // GLA_CHUNK
    q = q.reshape(B, H, n, GLA_CHUNK, dk)
    k = k.reshape(B, H, n, GLA_CHUNK, dk)
    v = v.reshape(B, H, n, GLA_CHUNK, dv)
    g = g.reshape(B, H, n, GLA_CHUNK, dk)
    G = jnp.cumsum(g, axis=3)
    qe = q * jnp.exp(G)
    ke = k * jnp.exp(-G)
    A = jnp.einsum('bhnid,bhnjd->bhnij', qe, ke)
    tril = jnp.tril(jnp.ones((GLA_CHUNK, GLA_CHUNK), dtype=bool))
    A = jnp.where(tril, A, 0.0)
    o_intra = jnp.einsum('bhnij,bhnjv->bhniv', A, v)
    G_last = G[:, :, :, -1]
    kd = k * jnp.exp(G_last[:, :, :, None, :] - G)

    def step(state, xs):
        qe_c, kd_c, v_c, gl_c = xs
        o = jnp.einsum('bhid,bhdv->bhiv', qe_c, state)
        state = jnp.exp(gl_c)[..., None] * state + jnp.einsum('bhjd,bhjv->bhdv', kd_c, v_c)
        return state, o

    state0 = jnp.zeros((B, H, dk, dv), jnp.float32)
    xs = (jnp.moveaxis(qe, 2, 0), jnp.moveaxis(kd, 2, 0), jnp.moveaxis(v, 2, 0), jnp.moveaxis(G_last, 2, 0))
    _, o_inter = lax.scan(step, state0, xs)
    o = o_intra + jnp.moveaxis(o_inter, 0, 2)
    return o.reshape(B, H, S, dv)


def _gla(h, w_in, w_gate_b, b_gate, gn_gain, w_o):
    B, S, _ = h.shape
    proj = h @ w_in
    q, k, v, r, ga = jnp.split(proj, [GLA_DK, 2 * GLA_DK, 2 * GLA_DK + GLA_DV, 2 * GLA_DK + 2 * GLA_DV], axis=-1)
    ga = ga.reshape(B, S, 2, GLA_RANK).astype(jnp.float32)
    glog = jax.nn.log_sigmoid(jnp.einsum('bsur,urk->bsuk', ga, w_gate_b.astype(jnp.float32))
                              + b_gate.astype(jnp.float32)) / GLA_GATE_NORM

    def heads(t, dh):
        return jnp.transpose(t.reshape(B, S, GLA_HEADS, dh), (0, 2, 1, 3)).astype(jnp.float32)

    qh = heads(q, GLA_DK_HEAD) * (GLA_DK_HEAD ** -0.5)
    kh = heads(k, GLA_DK_HEAD)
    vh = heads(v, GLA_DV_HEAD)
    g_f = heads(glog[:, :, 0], GLA_DK_HEAD)
    g_b = heads(glog[:, :, 1], GLA_DK_HEAD)
    o_f = _gla_chunked(qh, kh, vh, g_f)
    flip = lambda t: jnp.flip(t, axis=2)
    o_b = flip(_gla_chunked(flip(qh), flip(kh), flip(vh), flip(g_b)))
    o = o_f + o_b
    o = o * lax.rsqrt(jnp.mean(o * o, axis=-1, keepdims=True) + EPS) * gn_gain.astype(jnp.float32)
    o = jnp.transpose(o, (0, 2, 1, 3)).reshape(B, S, GLA_DV).astype(h.dtype)
    o = o * jax.nn.silu(r)
    return o @ w_o


def _trunk(x, norm_mix, norm_ffn, final_norm, nat_w_qkv, nat_rpb, nat_w_o,
           gla_w_in, gla_w_gate_b, gla_b_gate, gla_norm, gla_w_o, ffn_w_gu, ffn_w_down):
    for i in range(DEPTH):
        h = _rmsnorm(x, norm_mix[i])
        j = i // N_MIXERS
        if i % N_MIXERS == 0:
            mix = _natten(h, nat_w_qkv[j], nat_w_o[j], nat_rpb[j])
        else:
            mix = _gla(h, gla_w_in[j], gla_w_gate_b[j], gla_b_gate[j], gla_norm[j], gla_w_o[j])
        x = x + mix
        x = x + _swiglu(_rmsnorm(x, norm_ffn[i]), ffn_w_gu[i], ffn_w_down[i])
    return _rmsnorm(x, final_norm)


def setup_inputs(seed: int = 0) -> dict:
    key = jax.random.key(seed)
    ks = jax.random.split(key, 18)
    nrm = lambda k, shape, s: jax.random.normal(k, shape, jnp.float32) * s
    D = D_MODEL
    return {
        "x_prompt": nrm(ks[0], (BATCH, SEQ, D), 1.0),
        "x_sample": nrm(ks[1], (DEC_BATCH, DEC_SEQ, D), 1.0),
        "norm_mix": 1.0 + nrm(ks[2], (DEPTH, D), 0.02),
        "norm_ffn": 1.0 + nrm(ks[3], (DEPTH, D), 0.02),
        "final_norm": 1.0 + nrm(ks[4], (D,), 0.02),
        "nat_w_qkv": nrm(ks[5], (N_NAT, D, 3 * D), D ** -0.5),
        "nat_rpb": nrm(ks[6], (N_NAT, NAT_HEADS, 2 * WIN_H - 1, 2 * WIN_W - 1), 0.1),
        "nat_w_o": nrm(ks[7], (N_NAT, D, D), D ** -0.5),
        "gla_w_in": nrm(ks[8], (N_GLA, D, GLA_IN), D ** -0.5),
        "gla_w_gate_b": nrm(ks[9], (N_GLA, 2, GLA_RANK, GLA_DK), GLA_RANK ** -0.5),
        "gla_b_gate": nrm(ks[10], (N_GLA, 2, GLA_DK), 0.1),
        "gla_norm": 1.0 + nrm(ks[11], (N_GLA, GLA_DV_HEAD), 0.02),
        "gla_w_o": nrm(ks[12], (N_GLA, GLA_DV, D), GLA_DV ** -0.5),
        "ffn_w_gu": nrm(ks[13], (DEPTH, D, 2 * D_FF), D ** -0.5),
        "ffn_w_down": nrm(ks[14], (DEPTH, D_FF, D), D_FF ** -0.5),
    }


def reference(x_prompt, x_sample, norm_mix, norm_ffn, final_norm, nat_w_qkv, nat_rpb, nat_w_o,
              gla_w_in, gla_w_gate_b, gla_b_gate, gla_norm, gla_w_o, ffn_w_gu, ffn_w_down):
    y_prompt = _trunk(x_prompt, norm_mix, norm_ffn, final_norm, nat_w_qkv, nat_rpb, nat_w_o,
                      gla_w_in, gla_w_gate_b, gla_b_gate, gla_norm, gla_w_o, ffn_w_gu, ffn_w_down)
    y_sample = _trunk(x_sample, norm_mix, norm_ffn, final_norm, nat_w_qkv, nat_rpb, nat_w_o,
                      gla_w_in, gla_w_gate_b, gla_b_gate, gla_norm, gla_w_o, ffn_w_gu, ffn_w_down)
    return (y_prompt, y_sample)
```

```python
import functools

import jax
import jax.numpy as jnp
from jax import lax
from jax.experimental import pallas as pl
from jax.experimental.pallas import tpu as pltpu

GRID_W = 64
EPS = 1e-6
NAT_HEAD_DIM = 128
WIN_H = 8
WIN_W = 16
GLA_HEADS = 4
GLA_RANK = 16
GLA_GATE_NORM = 16.0
GLA_CHUNK = 64

V7X_VMEM_BYTES = 64 * 1024 * 1024
V7X_VMEM_REQUEST_CAP = 58 * 1024 * 1024
LANES = 128

NEG_BIG = -1e30
F32 = jnp.float32
BF16 = jnp.bfloat16


def _params(sem, est_bytes):
    limit = min(max(int(est_bytes * 1.25) + (4 << 20), 16 << 20), V7X_VMEM_REQUEST_CAP)
    return pltpu.CompilerParams(dimension_semantics=sem, vmem_limit_bytes=limit)


def _pick(n, prefs):
    for p in prefs:
        if n % p == 0:
            return p
    return n


def _rmsnorm_kernel(x_ref, g_ref, o_ref):
    x = x_ref[...]
    ms = jnp.mean(x * x, axis=-1, keepdims=True)
    o_ref[...] = (x * lax.rsqrt(ms + EPS) * g_ref[...]).astype(o_ref.dtype)


def _rmsnorm(x, g, out_dtype):
    m, d = x.shape
    tr = _pick(m, (256, 128, 64, 8))
    est = 2 * tr * d * 4 + 2 * tr * d * jnp.dtype(out_dtype).itemsize + 2 * tr * d * 4
    return pl.pallas_call(
        _rmsnorm_kernel,
        grid=(m // tr,),
        in_specs=[pl.BlockSpec((tr, d), lambda i: (i, 0)),
                  pl.BlockSpec((1, d), lambda i: (0, 0))],
        out_specs=pl.BlockSpec((tr, d), lambda i: (i, 0)),
        out_shape=jax.ShapeDtypeStruct((m, d), out_dtype),
        compiler_params=_params(("parallel",), est),
        name="rmsnorm",
    )(x, g.reshape(1, d).astype(F32))


def _mm_kernel(a_ref, w_ref, o_ref, *, scaled_blocks, scale):
    acc = jnp.dot(a_ref[...], w_ref[...], preferred_element_type=F32)
    if scaled_blocks:
        acc = acc * jnp.where(pl.program_id(1) < scaled_blocks, scale, 1.0).astype(F32)
    o_ref[...] = acc.astype(o_ref.dtype)


def _mm(a, w, out_dtype, *, scaled_cols=0, scale=1.0, tm_prefs=(1024, 512, 256, 128), tn_prefs=(1024, 512, 256, 128)):
    m, k = a.shape
    n = w.shape[1]
    tm = _pick(m, tm_prefs)
    tn = _pick(n, tuple(p for p in tn_prefs if scaled_cols % p == 0))
    assert scaled_cols % tn == 0
    ob = jnp.dtype(out_dtype).itemsize
    est = 2 * tm * k * 2 + 2 * k * tn * 2 + 2 * tm * tn * ob + 2 * tm * tn * 4
    return pl.pallas_call(
        functools.partial(_mm_kernel, scaled_blocks=scaled_cols // tn, scale=scale),
        grid=(m // tm, n // tn),
        in_specs=[pl.BlockSpec((tm, k), lambda i, j: (i, 0)),
                  pl.BlockSpec((k, tn), lambda i, j: (0, j))],
        out_specs=pl.BlockSpec((tm, tn), lambda i, j: (i, j)),
        out_shape=jax.ShapeDtypeStruct((m, n), out_dtype),
        compiler_params=_params(("parallel", "arbitrary"), est),
        name="matmul",
    )(a, w)


def _mm_res_kernel(a_ref, w_ref, r_ref, o_ref):
    o_ref[...] = r_ref[...] + jnp.dot(a_ref[...], w_ref[...], preferred_element_type=F32)


def _mm_res(a, w, res, *, tm_prefs, tn_prefs):
    m, k = a.shape
    n = w.shape[1]
    tm = _pick(m, tm_prefs)
    tn = _pick(n, tn_prefs)
    est = 2 * tm * k * 2 + 2 * k * tn * 2 + 4 * tm * tn * 4 + 2 * tm * tn * 4
    return pl.pallas_call(
        _mm_res_kernel,
        grid=(m // tm, n // tn),
        in_specs=[pl.BlockSpec((tm, k), lambda i, j: (i, 0)),
                  pl.BlockSpec((k, tn), lambda i, j: (0, j)),
                  pl.BlockSpec((tm, tn), lambda i, j: (i, j))],
        out_specs=pl.BlockSpec((tm, tn), lambda i, j: (i, j)),
        out_shape=jax.ShapeDtypeStruct((m, n), F32),
        compiler_params=_params(("parallel", "arbitrary"), est),
        name="matmul_residual",
    )(a, w, res)


def _swiglu_kernel(a_ref, wg_ref, wu_ref, o_ref):
    a = a_ref[...]
    g = jnp.dot(a, wg_ref[...], preferred_element_type=F32)
    u = jnp.dot(a, wu_ref[...], preferred_element_type=F32)
    o_ref[...] = (g * jax.nn.sigmoid(g) * u).astype(o_ref.dtype)


def _swiglu_up(a, w_gu):
    m, k = a.shape
    dff = w_gu.shape[1] // 2
    tm = _pick(m, (1024, 512, 256, 128))
    tn = _pick(dff, (256, 128))
    nb = dff // tn
    est = 2 * tm * k * 2 + 4 * k * tn * 2 + 2 * tm * tn * 2 + 4 * tm * tn * 4
    return pl.pallas_call(
        _swiglu_kernel,
        grid=(m // tm, nb),
        in_specs=[pl.BlockSpec((tm, k), lambda i, j: (i, 0)),
                  pl.BlockSpec((k, tn), lambda i, j: (0, j)),
                  pl.BlockSpec((k, tn), lambda i, j: (0, nb + j))],
        out_specs=pl.BlockSpec((tm, tn), lambda i, j: (i, j)),
        out_shape=jax.ShapeDtypeStruct((m, dff), BF16),
        compiler_params=_params(("parallel", "arbitrary"), est),
        name="swiglu_up",
    )(a, w_gu, w_gu)


N_BIAS_PAIRS = 2 * WIN_H - 2


def _bias_rows(rpb):
    h = rpb.shape[0]
    z = jnp.zeros((h, N_BIAS_PAIRS, GRID_W - 2 * WIN_W + 1), F32)
    lo = rpb[:, :N_BIAS_PAIRS]
    hi = rpb[:, 1:N_BIAS_PAIRS + 1]
    return jnp.concatenate([lo[:, :, WIN_W - 1:], z, hi, z, lo[:, :, :WIN_W - 1]], axis=-1).astype(F32)


def _nat_kernel(q_ref, k_ref, v_ref, base_ref, o_ref, bias_sc, *, rows):
    w = GRID_W
    qc = lax.broadcasted_iota(jnp.int32, (w, 2 * w), 0)
    kc = lax.broadcasted_iota(jnp.int32, (w, 2 * w), 1) & (w - 1)
    cs = jnp.clip(qc - WIN_W // 2, 0, w - WIN_W)
    valid = (kc >= cs) & (kc < cs + WIN_W)
    for p in range(N_BIAS_PAIRS):
        b = jnp.broadcast_to(base_ref[0, p:p + 1, :], (w, 2 * w))
        t = pltpu.roll(b, 0, 1, stride=1, stride_axis=0)
        bias_sc[p] = jnp.where(valid, t, NEG_BIG)

    kh = WIN_H
    nkeys = kh * w

    def body(r, carry):
        rs = jnp.clip(r - kh // 2, 0, rows - kh)
        var = r - rs
        q = q_ref[0, pl.ds(pl.multiple_of(r * w, w), w), :]
        koff = pl.multiple_of(rs * w, w)
        kw = k_ref[0, pl.ds(koff, nkeys), :]
        vw = v_ref[0, pl.ds(koff, nkeys), :]
        s = lax.dot_general(q, kw, (((1,), (1,)), ((), ())), preferred_element_type=F32)
        bias = jnp.concatenate([bias_sc[2 * g + (WIN_H - 1) - var] for g in range(kh // 2)], axis=1)
        s = s + bias
        m = jnp.max(s, axis=-1, keepdims=True)
        p = jnp.exp(s - m)
        l = jnp.sum(p, axis=-1, keepdims=True)
        o = jnp.dot(p.astype(BF16), vw, preferred_element_type=F32) / l
        o_ref[0, pl.ds(pl.multiple_of(r * w, w), w), :] = o.astype(o_ref.dtype)
        return carry

    lax.fori_loop(0, rows, body, 0)


def _natten(qkv, bias_rows, batch, seq):
    d = qkv.shape[1] // 3
    heads = d // NAT_HEAD_DIM
    rows = seq // GRID_W
    assert rows >= WIN_H and seq % GRID_W == 0
    qkv3 = qkv.reshape(batch, seq, 3 * d)
    hd = NAT_HEAD_DIM
    est = 8 * seq * hd * 2 + 2 * N_BIAS_PAIRS * LANES * 4 + N_BIAS_PAIRS * GRID_W * LANES * 4 + (4 << 20)
    out = pl.pallas_call(
        functools.partial(_nat_kernel, rows=rows),
        grid=(batch, heads),
        in_specs=[pl.BlockSpec((1, seq, hd), lambda b, h: (b, 0, h)),
                  pl.BlockSpec((1, seq, hd), lambda b, h: (b, 0, heads + h)),
                  pl.BlockSpec((1, seq, hd), lambda b, h: (b, 0, 2 * heads + h)),
                  pl.BlockSpec((1, N_BIAS_PAIRS, LANES), lambda b, h: (h, 0, 0))],
        out_specs=pl.BlockSpec((1, seq, hd), lambda b, h: (b, 0, h)),
        out_shape=jax.ShapeDtypeStruct((batch, seq, d), BF16),
        scratch_shapes=[pltpu.VMEM((N_BIAS_PAIRS, GRID_W, 2 * GRID_W), F32)],
        compiler_params=_params(("parallel", "arbitrary"), est),
        name="natten",
    )(qkv3, qkv3, qkv3, bias_rows)
    return out.reshape(batch * seq, d)


def _log_sigmoid(z):
    return jnp.minimum(z, 0.0) - jnp.log1p(jnp.exp(-jnp.abs(z)))


def _gla_chunk(q_ref, k_ref, v_ref, ga_ref, wb_ref, bg_ref, state_sc, *, reverse):
    c = GLA_CHUNK
    z = jnp.dot(ga_ref[0], wb_ref[...], preferred_element_type=F32, precision=lax.Precision.HIGHEST)
    g = _log_sigmoid(z + bg_ref[...]) * (1.0 / GLA_GATE_NORM)
    ri = lax.broadcasted_iota(jnp.int32, (c, c), 0)
    ci = lax.broadcasted_iota(jnp.int32, (c, c), 1)
    keep = (ci >= ri) if reverse else (ci <= ri)
    cum = jnp.dot(keep.astype(F32), g, preferred_element_type=F32, precision=lax.Precision.HIGHEST)
    g_end = cum[0:1, :] if reverse else cum[c - 1:c, :]
    q = q_ref[0].astype(F32)
    k = k_ref[0].astype(F32)
    v = v_ref[0]
    qe = (q * jnp.exp(cum)).astype(BF16)
    ke = (k * jnp.exp(-cum)).astype(BF16)
    kd = (k * jnp.exp(g_end - cum)).astype(BF16)
    a = lax.dot_general(qe, ke, (((1,), (1,)), ((), ())), preferred_element_type=F32)
    a = jnp.where(keep, a, 0.0).astype(BF16)
    st = state_sc[...]
    o = jnp.dot(a, v, preferred_element_type=F32)
    o = o + lax.dot_general(qe, st.astype(BF16), (((1,), (1,)), ((), ())), preferred_element_type=F32)
    upd = lax.dot_general(v, kd, (((0,), (0,)), ((), ())), preferred_element_type=F32)
    state_sc[...] = st * jnp.exp(g_end) + upd
    return o


def _gla_dir_kernel(q_ref, k_ref, v_ref, ga_ref, wb_ref, bg_ref, o_ref, state_sc, *, reverse):
    @pl.when(pl.program_id(2) == 0)
    def _():
        state_sc[...] = jnp.zeros_like(state_sc)

    o_ref[0] = _gla_chunk(q_ref, k_ref, v_ref, ga_ref, wb_ref, bg_ref, state_sc, reverse=reverse)


def _gla_final_kernel(q_ref, k_ref, v_ref, ga_ref, wb_ref, bg_ref, ob_ref, r_ref, gain_ref, o_ref, state_sc):
    @pl.when(pl.program_id(2) == 0)
    def _():
        state_sc[...] = jnp.zeros_like(state_sc)

    o = _gla_chunk(q_ref, k_ref, v_ref, ga_ref, wb_ref, bg_ref, state_sc, reverse=False) + ob_ref[0]
    o = o * lax.rsqrt(jnp.mean(o * o, axis=-1, keepdims=True) + EPS) * gain_ref[...]
    r = r_ref[0].astype(F32)
    o_ref[0] = (o * (r * jax.nn.sigmoid(r))).astype(o_ref.dtype)


def _gla_mix(proj, ga, w_gate_b, b_gate, gn_gain, batch, seq):
    dk = w_gate_b.shape[-1]
    dv = (proj.shape[1] - 2 * dk) // 2
    dkh, dvh = dk // GLA_HEADS, dv // GLA_HEADS
    c = GLA_CHUNK
    n = seq // c
    rk = GLA_RANK
    proj3 = proj.reshape(batch, seq, proj.shape[1])
    ga3 = ga.reshape(batch, seq, 2 * rk)
    zeros = jnp.zeros((rk, dk), F32)
    wb = (jnp.concatenate([w_gate_b[0].astype(F32), zeros], axis=0),
          jnp.concatenate([zeros, w_gate_b[1].astype(F32)], axis=0))
    bg = b_gate.astype(F32).reshape(2, 1, dk)
    gain = gn_gain.astype(F32).reshape(1, dvh)
    kq, kv = dk // dkh, (2 * dk) // dvh

    def specs(cidx):
        return [pl.BlockSpec((1, c, dkh), lambda b, h, i: (b, cidx(i), h)),
                pl.BlockSpec((1, c, dkh), lambda b, h, i: (b, cidx(i), kq + h)),
                pl.BlockSpec((1, c, dvh), lambda b, h, i: (b, cidx(i), kv + h)),
                pl.BlockSpec((1, c, 2 * rk), lambda b, h, i: (b, cidx(i), 0)),
                pl.BlockSpec((2 * rk, dkh), lambda b, h, i: (0, h)),
                pl.BlockSpec((1, dkh), lambda b, h, i: (0, h))]

    est = dvh * dkh * 4 * 4 + 8 * c * (2 * dkh + 3 * dvh) * 4 + (4 << 20)
    sem = ("parallel", "parallel", "arbitrary")
    rev = lambda i: n - 1 - i
    fwd = lambda i: i
    o_b = pl.pallas_call(
        functools.partial(_gla_dir_kernel, reverse=True),
        grid=(batch, GLA_HEADS, n),
        in_specs=specs(rev),
        out_specs=pl.BlockSpec((1, c, dvh), lambda b, h, i: (b, rev(i), h)),
        out_shape=jax.ShapeDtypeStruct((batch, seq, dv), F32),
        scratch_shapes=[pltpu.VMEM((dvh, dkh), F32)],
        compiler_params=_params(sem, est),
        name="gla_backward",
    )(proj3, proj3, proj3, ga3, wb[1], bg[1])
    out = pl.pallas_call(
        _gla_final_kernel,
        grid=(batch, GLA_HEADS, n),
        in_specs=specs(fwd) + [
            pl.BlockSpec((1, c, dvh), lambda b, h, i: (b, i, h)),
            pl.BlockSpec((1, c, dvh), lambda b, h, i: (b, i, kv + GLA_HEADS + h)),
            pl.BlockSpec((1, dvh), lambda b, h, i: (0, 0))],
        out_specs=pl.BlockSpec((1, c, dvh), lambda b, h, i: (b, i, h)),
        out_shape=jax.ShapeDtypeStruct((batch, seq, dv), BF16),
        scratch_shapes=[pltpu.VMEM((dvh, dkh), F32)],
        compiler_params=_params(sem, est),
        name="gla_forward_final",
    )(proj3, proj3, proj3, ga3, wb[0], bg[0], o_b, proj3, gain)
    return out.reshape(batch * seq, dv)


def _trunk(x, wts):
    batch, seq, d = x.shape
    x = x.reshape(batch * seq, d)
    depth = wts["norm_mix"].shape[0]
    for i in range(depth):
        h = _rmsnorm(x, wts["norm_mix"][i], BF16)
        j = i // 2
        if i % 2 == 0:
            qkv = _mm(h, wts["nat_w_qkv"][j], BF16, scaled_cols=d, scale=NAT_HEAD_DIM ** -0.5)
            mix = _natten(qkv, wts["nat_bias_rows"][j], batch, seq)
            x = _mm_res(mix, wts["nat_w_o"][j], x, tm_prefs=(1024, 512, 256, 128), tn_prefs=(512, 256, 128))
        else:
            dk = d // 2
            proj = _mm(h, wts["gla_w_main"][j], BF16, scaled_cols=dk, scale=(dk // GLA_HEADS) ** -0.5)
            ga = _mm(h, wts["gla_w_ga"][j], F32)
            mix = _gla_mix(proj, ga, wts["gla_w_gate_b"][j], wts["gla_b_gate"][j], wts["gla_norm"][j], batch, seq)
            x = _mm_res(mix, wts["gla_w_o"][j], x, tm_prefs=(1024, 512, 256, 128), tn_prefs=(512, 256, 128))
        h2 = _rmsnorm(x, wts["norm_ffn"][i], BF16)
        act = _swiglu_up(h2, wts["ffn_w_gu"][i])
        x = _mm_res(act, wts["ffn_w_down"][i], x, tm_prefs=(512, 256, 128), tn_prefs=(256, 128))
    y = _rmsnorm(x, wts["final_norm"], F32)
    return y.reshape(batch, seq, d)


def kernel(x_prompt, x_sample, norm_mix, norm_ffn, final_norm, nat_w_qkv, nat_rpb, nat_w_o,
           gla_w_in, gla_w_gate_b, gla_b_gate, gla_norm, gla_w_o, ffn_w_gu, ffn_w_down):
    d = x_prompt.shape[-1]
    n_main = 2 * (d // 2) + 2 * d
    wts = {
        "norm_mix": norm_mix, "norm_ffn": norm_ffn, "final_norm": final_norm,
        "nat_w_qkv": nat_w_qkv.astype(BF16),
        "nat_bias_rows": jax.vmap(_bias_rows)(nat_rpb.astype(F32)),
        "nat_w_o": nat_w_o.astype(BF16),
        "gla_w_main": gla_w_in[:, :, :n_main].astype(BF16),
        "gla_w_ga": gla_w_in[:, :, n_main:].astype(BF16),
        "gla_w_gate_b": gla_w_gate_b, "gla_b_gate": gla_b_gate, "gla_norm": gla_norm,
        "gla_w_o": gla_w_o.astype(BF16),
        "ffn_w_gu": ffn_w_gu.astype(BF16),
        "ffn_w_down": ffn_w_down.astype(BF16),
    }
    return (_trunk(x_prompt, wts), _trunk(x_sample, wts))
```

```python
import functools

import jax
import jax.numpy as jnp
from jax import lax
from jax.experimental import pallas as pl
from jax.experimental.pallas import tpu as pltpu

GRID_W = 64
EPS = 1e-6
NAT_HEAD_DIM = 128
WIN_H = 8
WIN_W = 16
GLA_HEADS = 4
GLA_RANK = 16
GLA_GATE_NORM = 16.0
GLA_CHUNK = 64
GLA_SUPER = 256
GLA_HEADS_PER_STEP = 2

V7X_VMEM_REQUEST_CAP = 58 * 1024 * 1024
LANES = 128

NEG_BIG = -1e30
F32 = jnp.float32
BF16 = jnp.bfloat16


def _params(sem, est_bytes):
    limit = min(max(int(est_bytes * 1.25) + (4 << 20), 16 << 20), V7X_VMEM_REQUEST_CAP)
    return pltpu.CompilerParams(dimension_semantics=sem, vmem_limit_bytes=limit)


def _pick(n, prefs):
    for p in prefs:
        if n % p == 0:
            return p
    return n


def _dot(a, b):
    return jnp.dot(a, b, preferred_element_type=F32)


def _dot_nt(a, b):
    return lax.dot_general(a, b, (((1,), (1,)), ((), ())), preferred_element_type=F32)


def _dot_tn(a, b):
    return lax.dot_general(a, b, (((0,), (0,)), ((), ())), preferred_element_type=F32)


def _lane_partial_sumsq(x):
    x2 = x * x
    acc = x2[:, :LANES]
    for t in range(1, x.shape[1] // LANES):
        acc = acc + x2[:, t * LANES:(t + 1) * LANES]
    return acc


def _rstd(ssq, d):
    return lax.rsqrt(jnp.sum(ssq, axis=-1, keepdims=True) * (1.0 / d) + EPS)


def _prep_kernel(x_ref, g_ref, xg_ref, ssq_ref):
    x = x_ref[...]
    xg_ref[...] = (x * g_ref[...]).astype(xg_ref.dtype)
    ssq_ref[...] = _lane_partial_sumsq(x)


def _prep_norm(x, g):
    m, d = x.shape
    tr = _pick(m, (256, 128, 64, 8))
    est = 4 * tr * d * 4 + 2 * tr * d * 2
    return pl.pallas_call(
        _prep_kernel,
        grid=(m // tr,),
        in_specs=[pl.BlockSpec((tr, d), lambda i: (i, 0)),
                  pl.BlockSpec((1, d), lambda i: (0, 0))],
        out_specs=[pl.BlockSpec((tr, d), lambda i: (i, 0)),
                   pl.BlockSpec((tr, LANES), lambda i: (i, 0))],
        out_shape=[jax.ShapeDtypeStruct((m, d), BF16),
                   jax.ShapeDtypeStruct((m, LANES), F32)],
        compiler_params=_params(("parallel",), est),
        name="norm_prep",
    )(x, g.reshape(1, d).astype(F32))


def _rmsnorm_kernel(x_ref, g_ref, o_ref):
    x = x_ref[...]
    ms = jnp.mean(x * x, axis=-1, keepdims=True)
    o_ref[...] = (x * lax.rsqrt(ms + EPS) * g_ref[...]).astype(o_ref.dtype)


def _rmsnorm(x, g, out_dtype):
    m, d = x.shape
    tr = _pick(m, (256, 128, 64, 8))
    est = 2 * tr * d * 4 + 2 * tr * d * jnp.dtype(out_dtype).itemsize + 2 * tr * d * 4
    return pl.pallas_call(
        _rmsnorm_kernel,
        grid=(m // tr,),
        in_specs=[pl.BlockSpec((tr, d), lambda i: (i, 0)),
                  pl.BlockSpec((1, d), lambda i: (0, 0))],
        out_specs=pl.BlockSpec((tr, d), lambda i: (i, 0)),
        out_shape=jax.ShapeDtypeStruct((m, d), out_dtype),
        compiler_params=_params(("parallel",), est),
        name="rmsnorm",
    )(x, g.reshape(1, d).astype(F32))


def _mm_norm_kernel(a_ref, ssq_ref, w_ref, o_ref, *, scaled_blocks, scale, d):
    acc = _dot(a_ref[...], w_ref[...]) * _rstd(ssq_ref[...], d)
    if scaled_blocks:
        acc = acc * jnp.where(pl.program_id(1) < scaled_blocks, scale, 1.0).astype(F32)
    o_ref[...] = acc.astype(o_ref.dtype)


def _mm_norm(xg, ssq, w, layer, out_dtype, *, scaled_cols=0, scale=1.0):
    m, k = xg.shape
    n = w.shape[2]
    tm = _pick(m, (1024, 512, 256, 128))
    tn = _pick(n, tuple(p for p in (1024, 512, 256, 128) if scaled_cols % p == 0))
    ob = jnp.dtype(out_dtype).itemsize
    est = 2 * tm * k * 2 + 2 * k * tn * 2 + 2 * tm * tn * ob + 2 * tm * tn * 4 + 2 * tm * LANES * 4
    return pl.pallas_call(
        functools.partial(_mm_norm_kernel, scaled_blocks=scaled_cols // tn, scale=scale, d=k),
        grid=(m // tm, n // tn),
        in_specs=[pl.BlockSpec((tm, k), lambda i, j: (i, 0)),
                  pl.BlockSpec((tm, LANES), lambda i, j: (i, 0)),
                  pl.BlockSpec((None, k, tn), lambda i, j: (layer, 0, j))],
        out_specs=pl.BlockSpec((tm, tn), lambda i, j: (i, j)),
        out_shape=jax.ShapeDtypeStruct((m, n), out_dtype),
        compiler_params=_params(("parallel", "arbitrary"), est),
        name="matmul_norm",
    )(xg, ssq, w)


def _mm_res_kernel(a_ref, w_ref, r_ref, g_ref, o_ref, xg_ref, ssq_ref):
    x = r_ref[...] + _dot(a_ref[...], w_ref[...])
    o_ref[...] = x
    xg_ref[...] = (x * g_ref[...]).astype(xg_ref.dtype)

    @pl.when(pl.program_id(1) == 0)
    def _():
        ssq_ref[...] = jnp.zeros_like(ssq_ref)

    ssq_ref[...] += _lane_partial_sumsq(x)


def _mm_res_plain_kernel(a_ref, w_ref, r_ref, o_ref):
    o_ref[...] = r_ref[...] + _dot(a_ref[...], w_ref[...])


def _mm_res(a, w, layer, res, gain, *, tm_prefs, tn_prefs):
    m, k = a.shape
    n = w.shape[2]
    tm = _pick(m, tm_prefs)
    tn = _pick(n, tn_prefs)
    est = 2 * tm * k * 2 + 2 * k * tn * 2 + 4 * tm * tn * 4 + 2 * tm * tn * 4 + 2 * tm * tn * 2 + 2 * tm * LANES * 4
    in_specs = [pl.BlockSpec((tm, k), lambda i, j: (i, 0)),
                pl.BlockSpec((None, k, tn), lambda i, j: (layer, 0, j)),
                pl.BlockSpec((tm, tn), lambda i, j: (i, j))]
    x_spec = pl.BlockSpec((tm, tn), lambda i, j: (i, j))
    x_shape = jax.ShapeDtypeStruct((m, n), F32)
    cp = _params(("parallel", "arbitrary"), est)
    if gain is None:
        return pl.pallas_call(
            _mm_res_plain_kernel, grid=(m // tm, n // tn), in_specs=in_specs, out_specs=x_spec,
            out_shape=x_shape, compiler_params=cp, name="matmul_residual",
        )(a, w, res)
    return pl.pallas_call(
        _mm_res_kernel,
        grid=(m // tm, n // tn),
        in_specs=in_specs + [pl.BlockSpec((1, tn), lambda i, j: (0, j))],
        out_specs=[x_spec,
                   pl.BlockSpec((tm, tn), lambda i, j: (i, j)),
                   pl.BlockSpec((tm, LANES), lambda i, j: (i, 0))],
        out_shape=[x_shape,
                   jax.ShapeDtypeStruct((m, n), BF16),
                   jax.ShapeDtypeStruct((m, LANES), F32)],
        compiler_params=cp,
        name="matmul_residual_norm",
    )(a, w, res, gain.reshape(1, n).astype(F32))


def _swiglu_kernel(a_ref, ssq_ref, wg_ref, wu_ref, o_ref, *, d):
    a = a_ref[...]
    rstd = _rstd(ssq_ref[...], d)
    g = _dot(a, wg_ref[...]) * rstd
    u = _dot(a, wu_ref[...]) * rstd
    o_ref[...] = (g * jax.nn.sigmoid(g) * u).astype(o_ref.dtype)


def _swiglu_up(xg, ssq, w_gu, layer):
    m, k = xg.shape
    dff = w_gu.shape[2] // 2
    tm = _pick(m, (1024, 512, 256, 128))
    tn = _pick(dff, (256, 128))
    nb = dff // tn
    est = 2 * tm * k * 2 + 4 * k * tn * 2 + 2 * tm * tn * 2 + 4 * tm * tn * 4 + 2 * tm * LANES * 4
    return pl.pallas_call(
        functools.partial(_swiglu_kernel, d=k),
        grid=(m // tm, nb),
        in_specs=[pl.BlockSpec((tm, k), lambda i, j: (i, 0)),
                  pl.BlockSpec((tm, LANES), lambda i, j: (i, 0)),
                  pl.BlockSpec((None, k, tn), lambda i, j: (layer, 0, j)),
                  pl.BlockSpec((None, k, tn), lambda i, j: (layer, 0, nb + j))],
        out_specs=pl.BlockSpec((tm, tn), lambda i, j: (i, j)),
        out_shape=jax.ShapeDtypeStruct((m, dff), BF16),
        compiler_params=_params(("parallel", "arbitrary"), est),
        name="swiglu_up",
    )(xg, ssq, w_gu, w_gu)


N_BIAS_PAIRS = 2 * WIN_H - 2
NAT_ROW_GROUP = 8


def _bias_rows(rpb):
    h = rpb.shape[0]
    z = jnp.zeros((h, N_BIAS_PAIRS, GRID_W - 2 * WIN_W + 1), F32)
    lo = rpb[:, :N_BIAS_PAIRS]
    hi = rpb[:, 1:N_BIAS_PAIRS + 1]
    return jnp.concatenate([lo[:, :, WIN_W - 1:], z, hi, z, lo[:, :, :WIN_W - 1]], axis=-1).astype(F32)


def _nat_kernel(q_ref, k_ref, v_ref, base_ref, o_ref, bias_sc, *, rows):
    w = GRID_W
    qc = lax.broadcasted_iota(jnp.int32, (w, 2 * w), 0)
    kc = lax.broadcasted_iota(jnp.int32, (w, 2 * w), 1) & (w - 1)
    cs = jnp.clip(qc - WIN_W // 2, 0, w - WIN_W)
    valid = (kc >= cs) & (kc < cs + WIN_W)
    for p in range(N_BIAS_PAIRS):
        b = jnp.broadcast_to(base_ref[0, p:p + 1, :], (w, 2 * w))
        t = pltpu.roll(b, 0, 1, stride=1, stride_axis=0)
        bias_sc[p] = jnp.where(valid, t, NEG_BIG)

    kh = WIN_H
    nkeys = kh * w

    def body(it, carry):
        rws = [it * NAT_ROW_GROUP + u for u in range(NAT_ROW_GROUP)]
        starts = [jnp.clip(r - kh // 2, 0, rows - kh) for r in rws]
        koffs = [pl.multiple_of(rs * w, w) for rs in starts]
        scores = []
        for r, rs, koff in zip(rws, starts, koffs):
            q = q_ref[0, pl.ds(pl.multiple_of(r * w, w), w), :]
            s = _dot_nt(q, k_ref[0, pl.ds(koff, nkeys), :])
            var = r - rs
            bias = jnp.concatenate([bias_sc[2 * g + (WIN_H - 1) - var] for g in range(kh // 2)], axis=1)
            scores.append(s + bias)
        probs, sums = [], []
        for s in scores:
            p = jnp.exp(s - jnp.max(s, axis=-1, keepdims=True))
            sums.append(jnp.sum(p, axis=-1, keepdims=True))
            probs.append(p.astype(BF16))
        outs = [_dot(p, v_ref[0, pl.ds(koff, nkeys), :]) / l for p, l, koff in zip(probs, sums, koffs)]
        for r, o in zip(rws, outs):
            o_ref[0, pl.ds(pl.multiple_of(r * w, w), w), :] = o.astype(o_ref.dtype)
        return carry

    lax.fori_loop(0, rows // NAT_ROW_GROUP, body, 0)


def _natten(qkv, bias_rows, batch, seq):
    d = qkv.shape[1] // 3
    heads = d // NAT_HEAD_DIM
    rows = seq // GRID_W
    assert rows >= WIN_H and seq % GRID_W == 0 and rows % NAT_ROW_GROUP == 0
    qkv3 = qkv.reshape(batch, seq, 3 * d)
    hd = NAT_HEAD_DIM
    est = 8 * seq * hd * 2 + 2 * N_BIAS_PAIRS * LANES * 4 + N_BIAS_PAIRS * GRID_W * LANES * 4 + (4 << 20)
    out = pl.pallas_call(
        functools.partial(_nat_kernel, rows=rows),
        grid=(batch, heads),
        in_specs=[pl.BlockSpec((1, seq, hd), lambda b, h: (b, 0, h)),
                  pl.BlockSpec((1, seq, hd), lambda b, h: (b, 0, heads + h)),
                  pl.BlockSpec((1, seq, hd), lambda b, h: (b, 0, 2 * heads + h)),
                  pl.BlockSpec((1, N_BIAS_PAIRS, LANES), lambda b, h: (h, 0, 0))],
        out_specs=pl.BlockSpec((1, seq, hd), lambda b, h: (b, 0, h)),
        out_shape=jax.ShapeDtypeStruct((batch, seq, d), BF16),
        scratch_shapes=[pltpu.VMEM((N_BIAS_PAIRS, GRID_W, 2 * GRID_W), F32)],
        compiler_params=_params(("parallel", "arbitrary"), est),
        name="natten",
    )(qkv3, qkv3, qkv3, bias_rows)
    return out.reshape(batch * seq, d)


def _log_sigmoid(z):
    return jnp.minimum(z, 0.0) - jnp.log(1.0 + jnp.exp(-jnp.abs(z)))


def _bf16_terms(x, n):
    out = []
    for _ in range(n - 1):
        t = x.astype(BF16)
        out.append(t)
        x = x - t.astype(F32)
    out.append(x.astype(BF16))
    return out


def _gla_masks(reverse):
    cs, c = GLA_SUPER, GLA_CHUNK
    ri = lax.broadcasted_iota(jnp.int32, (cs, cs), 0)
    ci = lax.broadcasted_iota(jnp.int32, (cs, cs), 1)
    tri = (ci >= ri) if reverse else (ci <= ri)
    same = (ri & -c) == (ci & -c)
    mblk = jnp.where(same & tri, 1.0, 0.0).astype(BF16)
    r2 = lax.broadcasted_iota(jnp.int32, (2 * c, 2 * c), 0)
    c2 = lax.broadcasted_iota(jnp.int32, (2 * c, 2 * c), 1)
    keep2 = (c2 >= r2) if reverse else (c2 <= r2)
    return mblk, keep2


def _gla_super(q, k, v, ga, w_hi, w_lo, bg, st_ref, masks, *, reverse):
    cs, c = GLA_SUPER, GLA_CHUNK
    nb = cs // c
    assert nb == 4
    dk = q.shape[-1]
    mblk, keep2 = masks

    a_hi, a_lo = _bf16_terms(ga, 2)
    z = _dot(a_hi, w_hi) + _dot(a_hi, w_lo) + _dot(a_lo, w_hi) + bg
    g = _log_sigmoid(z) * (1.0 / GLA_GATE_NORM)
    yield
    gl = sum(_dot(mblk, t) for t in _bf16_terms(g, 3))
    yield

    def rows(x, b):
        return x[b * c:(b + 1) * c]

    tot = [gl[b * c:b * c + 1] if reverse else gl[(b + 1) * c - 1:(b + 1) * c] for b in range(nb)]
    tb = jnp.concatenate([jnp.broadcast_to(t, (c, dk)) for t in tot], axis=0)
    dloc = gl - tb
    qd = q.astype(F32) * jnp.exp(dloc)
    kd = k.astype(F32) * jnp.exp(-dloc)
    yield

    order = list(range(nb - 1, -1, -1)) if reverse else list(range(nb))
    ex = jnp.exp

    def cat_tok(blocks, axis=0):
        return jnp.concatenate([blocks[b] for b in sorted(blocks)], axis=axis)

    halves = [(order[0], order[1]), (order[2], order[3])]
    a_half, k_late = [], []
    for b0, b1 in halves:
        kc0 = cat_tok({b0: rows(kd, b0), b1: rows(kd, b1)}).astype(BF16)
        kc1 = cat_tok({b0: rows(kd, b0) * ex(tot[b1]), b1: rows(kd, b1)}).astype(BF16)
        s0 = _dot_nt(rows(qd, b0).astype(BF16), kc0)
        s1 = _dot_nt(rows(qd, b1).astype(BF16), kc1)
        a = cat_tok({b0: s0, b1: s1})
        a_half.append(jnp.where(keep2, a, 0.0).astype(BF16))
        k_late.append(kc1)
        yield
    (b0, b1), (b2, b3) = halves
    qx = cat_tok({b2: rows(qd, b2) * ex(tot[b2]), b3: rows(qd, b3) * ex(tot[b3] + tot[b2])}).astype(BF16)
    a_cross = _dot_nt(qx, k_late[0]).astype(BF16)
    yield

    lo0, lo1 = min(b0, b1) * c, min(b2, b3) * c
    v0, v1 = v[lo0:lo0 + 2 * c], v[lo1:lo1 + 2 * c]
    o0 = _dot(a_half[0], v0)
    o1 = _dot(a_half[1], v1) + _dot(a_cross, v0)
    yield

    before, after = {}, {}
    run = None
    for b in order:
        before[b] = run
        run = tot[b] if run is None else run + tot[b]
    total = run
    run = None
    for b in reversed(order):
        after[b] = run
        run = tot[b] if run is None else run + tot[b]
    qs = cat_tok({b: rows(qd, b) * ex(tot[b] if before[b] is None else tot[b] + before[b]) for b in order})
    ku = cat_tok({b: rows(kd, b) if after[b] is None else rows(kd, b) * ex(after[b]) for b in order})
    st = st_ref[...]
    o_inter = _dot_nt(qs.astype(BF16), st.astype(BF16))
    yield
    st_ref[...] = st * ex(total) + _dot_tn(v, ku.astype(BF16))
    o_intra = jnp.concatenate([o0, o1] if lo0 < lo1 else [o1, o0], axis=0)
    return o_intra + o_inter


def _run_interleaved(gens):
    results = [None] * len(gens)
    live = list(enumerate(gens))
    while live:
        still = []
        for idx, gen in live:
            try:
                next(gen)
                still.append((idx, gen))
            except StopIteration as stop:
                results[idx] = stop.value
        live = still
    return results


def _gla_heads(q_ref, k_ref, v_ref, ga_ref, wbh_ref, wbl_ref, bg_ref, state_sc, *, reverse):
    nh = state_sc.shape[0]
    dvh, dkh = state_sc.shape[1], state_sc.shape[2]

    @pl.when(pl.program_id(2) == 0)
    def _():
        state_sc[...] = jnp.zeros_like(state_sc)

    masks = _gla_masks(reverse)
    ga = ga_ref[0]
    gens = []
    for u in range(nh):
        ksl = slice(u * dkh, (u + 1) * dkh)
        vsl = slice(u * dvh, (u + 1) * dvh)
        gens.append(_gla_super(q_ref[0, :, ksl], k_ref[0, :, ksl], v_ref[0, :, vsl], ga,
                               wbh_ref[:, ksl], wbl_ref[:, ksl], bg_ref[:, ksl], state_sc.at[u], masks,
                               reverse=reverse))
    return _run_interleaved(gens)


def _gla_dir_kernel(q_ref, k_ref, v_ref, ga_ref, wbh_ref, wbl_ref, bg_ref, o_ref, state_sc, *, reverse):
    dvh = state_sc.shape[1]
    outs = _gla_heads(q_ref, k_ref, v_ref, ga_ref, wbh_ref, wbl_ref, bg_ref, state_sc, reverse=reverse)
    for u, o in enumerate(outs):
        o_ref[0, :, u * dvh:(u + 1) * dvh] = o


def _gla_final_kernel(q_ref, k_ref, v_ref, ga_ref, wbh_ref, wbl_ref, bg_ref, ob_ref, r_ref, gain_ref,
                      o_ref, state_sc):
    dvh = state_sc.shape[1]
    outs = _gla_heads(q_ref, k_ref, v_ref, ga_ref, wbh_ref, wbl_ref, bg_ref, state_sc, reverse=False)
    for u, o in enumerate(outs):
        vsl = slice(u * dvh, (u + 1) * dvh)
        o = o + ob_ref[0, :, vsl]
        o = o * lax.rsqrt(jnp.mean(o * o, axis=-1, keepdims=True) + EPS) * gain_ref[...]
        r = r_ref[0, :, vsl].astype(F32)
        o_ref[0, :, vsl] = (o * (r * jax.nn.sigmoid(r))).astype(o_ref.dtype)


def _gla_mix(proj, ga, w_gate_b, b_gate, gn_gain, batch, seq):
    dk = w_gate_b.shape[-1]
    dv = (proj.shape[1] - 2 * dk) // 2
    dkh, dvh = dk // GLA_HEADS, dv // GLA_HEADS
    nh = GLA_HEADS_PER_STEP
    hk, hv = nh * dkh, nh * dvh
    cs = GLA_SUPER
    assert seq % cs == 0 and GLA_HEADS % nh == 0
    n = seq // cs
    rk = GLA_RANK
    proj3 = proj.reshape(batch, seq, proj.shape[1])
    ga3 = ga.reshape(batch, seq, 2 * rk)
    zeros = jnp.zeros((rk, dk), F32)
    wb = (jnp.concatenate([w_gate_b[0].astype(F32), zeros], axis=0),
          jnp.concatenate([zeros, w_gate_b[1].astype(F32)], axis=0))
    wb_hi = [w.astype(BF16) for w in wb]
    wb_lo = [(w - h.astype(F32)).astype(BF16) for w, h in zip(wb, wb_hi)]
    bg = b_gate.astype(F32).reshape(2, 1, dk)
    gain = gn_gain.astype(F32).reshape(1, dvh)
    kq, kv = dk // hk, (2 * dk) // hv

    def specs(cidx):
        return [pl.BlockSpec((1, cs, hk), lambda b, h, i: (b, cidx(i), h)),
                pl.BlockSpec((1, cs, hk), lambda b, h, i: (b, cidx(i), kq + h)),
                pl.BlockSpec((1, cs, hv), lambda b, h, i: (b, cidx(i), kv + h)),
                pl.BlockSpec((1, cs, 2 * rk), lambda b, h, i: (b, cidx(i), 0)),
                pl.BlockSpec((2 * rk, hk), lambda b, h, i: (0, h)),
                pl.BlockSpec((2 * rk, hk), lambda b, h, i: (0, h)),
                pl.BlockSpec((1, hk), lambda b, h, i: (0, h))]

    est = nh * (dvh * dkh * 4 * 3 + 8 * cs * (2 * dkh + 3 * dvh) * 4 + 24 * cs * dkh * 4) + (4 << 20)
    sem = ("parallel", "parallel", "arbitrary")
    grid = (batch, GLA_HEADS // nh, n)
    state = [pltpu.VMEM((nh, dvh, dkh), F32)]
    rev = lambda i: n - 1 - i
    fwd = lambda i: i
    o_b = pl.pallas_call(
        functools.partial(_gla_dir_kernel, reverse=True),
        grid=grid,
        in_specs=specs(rev),
        out_specs=pl.BlockSpec((1, cs, hv), lambda b, h, i: (b, rev(i), h)),
        out_shape=jax.ShapeDtypeStruct((batch, seq, dv), F32),
        scratch_shapes=state,
        compiler_params=_params(sem, est),
        name="gla_backward",
    )(proj3, proj3, proj3, ga3, wb_hi[1], wb_lo[1], bg[1])
    out = pl.pallas_call(
        _gla_final_kernel,
        grid=grid,
        in_specs=specs(fwd) + [
            pl.BlockSpec((1, cs, hv), lambda b, h, i: (b, i, h)),
            pl.BlockSpec((1, cs, hv), lambda b, h, i: (b, i, kv + GLA_HEADS // nh + h)),
            pl.BlockSpec((1, dvh), lambda b, h, i: (0, 0))],
        out_specs=pl.BlockSpec((1, cs, hv), lambda b, h, i: (b, i, h)),
        out_shape=jax.ShapeDtypeStruct((batch, seq, dv), BF16),
        scratch_shapes=state,
        compiler_params=_params(sem, est),
        name="gla_forward_final",
    )(proj3, proj3, proj3, ga3, wb_hi[0], wb_lo[0], bg[0], o_b, proj3, gain)
    return out.reshape(batch * seq, dv)


def _trunk(x, wts):
    batch, seq, d = x.shape
    x = x.reshape(batch * seq, d)
    depth = wts["norm_mix"].shape[0]
    big = (1024, 512, 256, 128)
    xg, ssq = _prep_norm(x, wts["norm_mix"][0])
    for i in range(depth):
        j = i // 2
        if i % 2 == 0:
            qkv = _mm_norm(xg, ssq, wts["nat_w_qkv"], j, BF16, scaled_cols=d, scale=NAT_HEAD_DIM ** -0.5)
            mix = _natten(qkv, wts["nat_bias_rows"][j], batch, seq)
            w_o = wts["nat_w_o"]
        else:
            dk = d // 2
            proj = _mm_norm(xg, ssq, wts["gla_w_main"], j, BF16, scaled_cols=dk, scale=(dk // GLA_HEADS) ** -0.5)
            ga = _mm_norm(xg, ssq, wts["gla_w_ga"], j, F32)
            mix = _gla_mix(proj, ga, wts["gla_w_gate_b"][j], wts["gla_b_gate"][j], wts["gla_norm"][j], batch, seq)
            w_o = wts["gla_w_o"]
        x, xg, ssq = _mm_res(mix, w_o, j, x, wts["norm_ffn"][i], tm_prefs=big, tn_prefs=(512, 256, 128))
        act = _swiglu_up(xg, ssq, wts["ffn_w_gu"], i)
        if i + 1 < depth:
            x, xg, ssq = _mm_res(act, wts["ffn_w_down"], i, x, wts["norm_mix"][i + 1],
                                 tm_prefs=(512, 256, 128), tn_prefs=(256, 128))
        else:
            x = _mm_res(act, wts["ffn_w_down"], i, x, None, tm_prefs=(512, 256, 128), tn_prefs=(256, 128))
    y = _rmsnorm(x, wts["final_norm"], F32)
    return y.reshape(batch, seq, d)


def kernel(x_prompt, x_sample, norm_mix, norm_ffn, final_norm, nat_w_qkv, nat_rpb, nat_w_o,
           gla_w_in, gla_w_gate_b, gla_b_gate, gla_norm, gla_w_o, ffn_w_gu, ffn_w_down):
    d = x_prompt.shape[-1]
    n_main = 2 * (d // 2) + 2 * d
    wts = {
        "norm_mix": norm_mix, "norm_ffn": norm_ffn, "final_norm": final_norm,
        "nat_w_qkv": nat_w_qkv.astype(BF16),
        "nat_bias_rows": jax.vmap(_bias_rows)(nat_rpb.astype(F32)),
        "nat_w_o": nat_w_o.astype(BF16),
        "gla_w_main": gla_w_in[:, :, :n_main].astype(BF16),
        "gla_w_ga": gla_w_in[:, :, n_main:].astype(BF16),
        "gla_w_gate_b": gla_w_gate_b, "gla_b_gate": gla_b_gate, "gla_norm": gla_norm,
        "gla_w_o": gla_w_o.astype(BF16),
        "ffn_w_gu": ffn_w_gu.astype(BF16),
        "ffn_w_down": ffn_w_down.astype(BF16),
    }
    return (_trunk(x_prompt, wts), _trunk(x_sample, wts))
```

```python
import functools

import jax
import jax.numpy as jnp
from jax import lax
from jax.experimental import pallas as pl
from jax.experimental.pallas import tpu as pltpu

GRID_W = 64
EPS = 1e-6
NAT_HEAD_DIM = 128
WIN_H = 8
WIN_W = 16
GLA_HEADS = 4
GLA_RANK = 16
GLA_GATE_NORM = 16.0
GLA_CHUNK = 64
GLA_SUPER = 256
GLA_HEADS_PER_STEP = 2

V7X_VMEM_REQUEST_CAP = 58 * 1024 * 1024
LANES = 128

NEG_BIG = -1e30
F32 = jnp.float32
BF16 = jnp.bfloat16


def _params(sem, est_bytes):
    limit = min(max(int(est_bytes * 1.25) + (4 << 20), 16 << 20), V7X_VMEM_REQUEST_CAP)
    return pltpu.CompilerParams(dimension_semantics=sem, vmem_limit_bytes=limit)


def _pick(n, prefs):
    for p in prefs:
        if n % p == 0:
            return p
    return n


def _dot(a, b):
    return jnp.dot(a, b, preferred_element_type=F32)


def _dot_nt(a, b):
    return lax.dot_general(a, b, (((1,), (1,)), ((), ())), preferred_element_type=F32)


def _dot_tn(a, b):
    return lax.dot_general(a, b, (((0,), (0,)), ((), ())), preferred_element_type=F32)


def _lane_partial_sumsq(x):
    x2 = x * x
    acc = x2[:, :LANES]
    for t in range(1, x.shape[1] // LANES):
        acc = acc + x2[:, t * LANES:(t + 1) * LANES]
    return acc


def _rstd(ssq_ref, d):
    s = ssq_ref[0]
    for p in range(1, ssq_ref.shape[0]):
        s = s + ssq_ref[p]
    return lax.rsqrt(jnp.sum(s, axis=-1, keepdims=True) * (1.0 / d) + EPS)


def _prep_kernel(x_ref, g_ref, xg_ref, ssq_ref):
    x = x_ref[...]
    xg_ref[...] = (x * g_ref[...]).astype(xg_ref.dtype)
    ssq_ref[...] = _lane_partial_sumsq(x)


def _prep_norm(x, g):
    m, d = x.shape
    tr = _pick(m, (256, 128, 64, 8))
    est = 4 * tr * d * 4 + 2 * tr * d * 2
    return pl.pallas_call(
        _prep_kernel,
        grid=(m // tr,),
        in_specs=[pl.BlockSpec((tr, d), lambda i: (i, 0)),
                  pl.BlockSpec((1, d), lambda i: (0, 0))],
        out_specs=[pl.BlockSpec((tr, d), lambda i: (i, 0)),
                   pl.BlockSpec((None, tr, LANES), lambda i: (0, i, 0))],
        out_shape=[jax.ShapeDtypeStruct((m, d), BF16),
                   jax.ShapeDtypeStruct((1, m, LANES), F32)],
        compiler_params=_params(("parallel",), est),
        name="norm_prep",
    )(x, g.reshape(1, d).astype(F32))


def _rmsnorm_kernel(x_ref, g_ref, o_ref):
    x = x_ref[...]
    ms = jnp.mean(x * x, axis=-1, keepdims=True)
    o_ref[...] = (x * lax.rsqrt(ms + EPS) * g_ref[...]).astype(o_ref.dtype)


def _rmsnorm(x, g, out_dtype):
    m, d = x.shape
    tr = _pick(m, (256, 128, 64, 8))
    est = 2 * tr * d * 4 + 2 * tr * d * jnp.dtype(out_dtype).itemsize + 2 * tr * d * 4
    return pl.pallas_call(
        _rmsnorm_kernel,
        grid=(m // tr,),
        in_specs=[pl.BlockSpec((tr, d), lambda i: (i, 0)),
                  pl.BlockSpec((1, d), lambda i: (0, 0))],
        out_specs=pl.BlockSpec((tr, d), lambda i: (i, 0)),
        out_shape=jax.ShapeDtypeStruct((m, d), out_dtype),
        compiler_params=_params(("parallel",), est),
        name="rmsnorm",
    )(x, g.reshape(1, d).astype(F32))


def _mm_norm_kernel(a_ref, ssq_ref, w_ref, o_ref, *, scaled_blocks, scale, d, head_major):
    acc = _dot(a_ref[...], w_ref[...]) * _rstd(ssq_ref, d)
    if scaled_blocks:
        acc = acc * jnp.where(pl.program_id(1) < scaled_blocks, scale, 1.0).astype(F32)
    if head_major:
        for h in range(o_ref.shape[0]):
            o_ref[h] = acc[:, h * LANES:(h + 1) * LANES].astype(o_ref.dtype)
    else:
        o_ref[...] = acc.astype(o_ref.dtype)


def _mm_norm(xg, ssq, w, layer, out_dtype, *, scaled_cols=0, scale=1.0, head_major=False):
    m, k = xg.shape
    n = w.shape[2]
    tm = _pick(m, (1024, 512, 256, 128))
    tn = _pick(n, tuple(p for p in (1024, 512, 256, 128) if scaled_cols % p == 0))
    ob = jnp.dtype(out_dtype).itemsize
    ps = ssq.shape[0]
    est = 2 * tm * k * 2 + 2 * k * tn * 2 + 2 * tm * tn * ob + 2 * tm * tn * 4 + 2 * ps * tm * LANES * 4
    if head_major:
        out_spec = pl.BlockSpec((tn // LANES, tm, LANES), lambda i, j: (j, i, 0))
        out_shape = jax.ShapeDtypeStruct((n // LANES, m, LANES), out_dtype)
    else:
        out_spec = pl.BlockSpec((tm, tn), lambda i, j: (i, j))
        out_shape = jax.ShapeDtypeStruct((m, n), out_dtype)
    return pl.pallas_call(
        functools.partial(_mm_norm_kernel, scaled_blocks=scaled_cols // tn, scale=scale, d=k,
                          head_major=head_major),
        grid=(m // tm, n // tn),
        in_specs=[pl.BlockSpec((tm, k), lambda i, j: (i, 0)),
                  pl.BlockSpec((ps, tm, LANES), lambda i, j: (0, i, 0)),
                  pl.BlockSpec((None, k, tn), lambda i, j: (layer, 0, j))],
        out_specs=out_spec,
        out_shape=out_shape,
        compiler_params=_params(("parallel", "arbitrary"), est),
        name="matmul_norm",
    )(xg, ssq, w)


def _res_epilogue(x, g_ref, o_ref, xg_ref):
    o_ref[...] = x
    xg_ref[...] = (x * g_ref[...]).astype(xg_ref.dtype)


def _mm_res_kernel(a_ref, w_ref, r_ref, g_ref, o_ref, xg_ref, ssq_ref, *a_scratch):
    if a_scratch:
        a_sc, = a_scratch

        @pl.when(pl.program_id(1) == 0)
        def _():
            for h in range(a_ref.shape[0]):
                a_sc[:, h * LANES:(h + 1) * LANES] = a_ref[h]

        a = a_sc[...]
    else:
        a = a_ref[...]
    x = r_ref[...] + _dot(a, w_ref[...])
    _res_epilogue(x, g_ref, o_ref, xg_ref)

    @pl.when(pl.program_id(1) == 0)
    def _():
        ssq_ref[...] = jnp.zeros_like(ssq_ref)

    ssq_ref[...] += _lane_partial_sumsq(x)


def _mm_res(a, w, layer, res, gain):
    grouped = a.ndim == 3
    m = a.shape[1] if grouped else a.shape[0]
    k, n = w.shape[1], w.shape[2]
    tm = _pick(m, (1024, 512, 256, 128))
    tn = _pick(n, (512, 256, 128))
    est = (3 if grouped else 2) * tm * k * 2 + 2 * k * tn * 2 + 6 * tm * tn * 4 + 2 * tm * tn * 2 + 2 * tm * LANES * 4
    a_spec = (pl.BlockSpec((k // LANES, tm, LANES), lambda i, j: (0, i, 0)) if grouped
              else pl.BlockSpec((tm, k), lambda i, j: (i, 0)))
    tile = lambda: pl.BlockSpec((tm, tn), lambda i, j: (i, j))
    return pl.pallas_call(
        _mm_res_kernel,
        grid=(m // tm, n // tn),
        in_specs=[a_spec,
                  pl.BlockSpec((None, k, tn), lambda i, j: (layer, 0, j)),
                  tile(),
                  pl.BlockSpec((1, tn), lambda i, j: (0, j))],
        out_specs=[tile(), tile(), pl.BlockSpec((None, tm, LANES), lambda i, j: (0, i, 0))],
        out_shape=[jax.ShapeDtypeStruct((m, n), F32),
                   jax.ShapeDtypeStruct((m, n), BF16),
                   jax.ShapeDtypeStruct((1, m, LANES), F32)],
        scratch_shapes=[pltpu.VMEM((tm, k), BF16)] if grouped else [],
        compiler_params=_params(("parallel", "arbitrary"), est),
        name="matmul_residual_norm",
    )(a, w, res, gain.reshape(1, n).astype(F32))


def _mm_res_wres_kernel(a_ref, w_ref, r_ref, g_ref, o_ref, xg_ref, ssq_ref):
    x = r_ref[...] + _dot(a_ref[...], w_ref[...])
    _res_epilogue(x, g_ref, o_ref, xg_ref)
    ssq_ref[...] = _lane_partial_sumsq(x)


def _mm_res_wres_plain_kernel(a_ref, w_ref, r_ref, o_ref):
    o_ref[...] = r_ref[...] + _dot(a_ref[...], w_ref[...])


def _mm_res_wres(a, w, layer, res, gain):
    m, k = a.shape
    n = w.shape[2]
    tm = _pick(m, (512, 256, 128))
    tn = _pick(n, (512, 256, 128))
    est = 2 * tm * k * 2 + k * tn * 2 + 6 * tm * tn * 4 + 2 * tm * tn * 2 + 2 * tm * LANES * 4
    tile = lambda: pl.BlockSpec((tm, tn), lambda j, i: (i, j))
    in_specs = [pl.BlockSpec((tm, k), lambda j, i: (i, 0)),
                pl.BlockSpec((None, k, tn), lambda j, i: (layer, 0, j), pipeline_mode=pl.Buffered(1)),
                tile()]
    x_shape = jax.ShapeDtypeStruct((m, n), F32)
    cp = _params(("arbitrary", "arbitrary"), est)
    grid = (n // tn, m // tm)
    if gain is None:
        return pl.pallas_call(
            _mm_res_wres_plain_kernel, grid=grid, in_specs=in_specs, out_specs=tile(),
            out_shape=x_shape, compiler_params=cp, name="matmul_residual_wres",
        )(a, w, res)
    return pl.pallas_call(
        _mm_res_wres_kernel,
        grid=grid,
        in_specs=in_specs + [pl.BlockSpec((1, tn), lambda j, i: (0, j))],
        out_specs=[tile(), tile(), pl.BlockSpec((None, tm, LANES), lambda j, i: (j, i, 0))],
        out_shape=[x_shape,
                   jax.ShapeDtypeStruct((m, n), BF16),
                   jax.ShapeDtypeStruct((n // tn, m, LANES), F32)],
        compiler_params=cp,
        name="matmul_residual_wres_norm",
    )(a, w, res, gain.reshape(1, n).astype(F32))


def _swiglu_kernel(a_ref, ssq_ref, wg_ref, wu_ref, o_ref, *, d):
    a = a_ref[...]
    rstd = _rstd(ssq_ref, d)
    g = _dot(a, wg_ref[...]) * rstd
    u = _dot(a, wu_ref[...]) * rstd
    o_ref[...] = (g * jax.nn.sigmoid(g) * u).astype(o_ref.dtype)


def _swiglu_up(xg, ssq, w_gu, layer):
    m, k = xg.shape
    dff = w_gu.shape[2] // 2
    tm = _pick(m, (2048, 1024, 512, 256, 128))
    tn = _pick(dff, (256, 128))
    nb = dff // tn
    ps = ssq.shape[0]
    est = 2 * tm * k * 2 + 4 * k * tn * 2 + 2 * tm * tn * 2 + 4 * tm * tn * 4 + 2 * ps * tm * LANES * 4
    return pl.pallas_call(
        functools.partial(_swiglu_kernel, d=k),
        grid=(m // tm, nb),
        in_specs=[pl.BlockSpec((tm, k), lambda i, j: (i, 0)),
                  pl.BlockSpec((ps, tm, LANES), lambda i, j: (0, i, 0)),
                  pl.BlockSpec((None, k, tn), lambda i, j: (layer, 0, j)),
                  pl.BlockSpec((None, k, tn), lambda i, j: (layer, 0, nb + j))],
        out_specs=pl.BlockSpec((tm, tn), lambda i, j: (i, j)),
        out_shape=jax.ShapeDtypeStruct((m, dff), BF16),
        compiler_params=_params(("parallel", "arbitrary"), est),
        name="swiglu_up",
    )(xg, ssq, w_gu, w_gu)


N_BIAS_PAIRS = 2 * WIN_H - 2
NAT_ROW_GROUP = 16


def _bias_rows(rpb):
    h = rpb.shape[0]
    z = jnp.zeros((h, N_BIAS_PAIRS, GRID_W - 2 * WIN_W + 1), F32)
    lo = rpb[:, :N_BIAS_PAIRS]
    hi = rpb[:, 1:N_BIAS_PAIRS + 1]
    return jnp.concatenate([lo[:, :, WIN_W - 1:], z, hi, z, lo[:, :, :WIN_W - 1]], axis=-1).astype(F32)


def _nat_kernel(q_ref, k_ref, v_ref, base_ref, o_ref, bias_sc, *, rows):
    w = GRID_W
    qc = lax.broadcasted_iota(jnp.int32, (w, 2 * w), 0)
    kc = lax.broadcasted_iota(jnp.int32, (w, 2 * w), 1) & (w - 1)
    cs = jnp.clip(qc - WIN_W // 2, 0, w - WIN_W)
    valid = (kc >= cs) & (kc < cs + WIN_W)
    for p in range(N_BIAS_PAIRS):
        b = jnp.broadcast_to(base_ref[0, p:p + 1, :], (w, 2 * w))
        t = pltpu.roll(b, 0, 1, stride=1, stride_axis=0)
        bias_sc[p] = jnp.where(valid, t, NEG_BIG)

    kh = WIN_H
    nkeys = kh * w

    def body(it, carry):
        rws = [it * NAT_ROW_GROUP + u for u in range(NAT_ROW_GROUP)]
        starts = [jnp.clip(r - kh // 2, 0, rows - kh) for r in rws]
        koffs = [pl.multiple_of(rs * w, w) for rs in starts]
        scores = []
        for r, rs, koff in zip(rws, starts, koffs):
            q = q_ref[0, pl.ds(pl.multiple_of(r * w, w), w), :]
            s = _dot_nt(q, k_ref[0, pl.ds(koff, nkeys), :])
            var = r - rs
            bias = jnp.concatenate([bias_sc[2 * g + (WIN_H - 1) - var] for g in range(kh // 2)], axis=1)
            scores.append(s + bias)
        probs, sums = [], []
        for s in scores:
            p = jnp.exp(s - jnp.max(s, axis=-1, keepdims=True))
            sums.append(jnp.sum(p, axis=-1, keepdims=True))
            probs.append(p.astype(BF16))
        outs = [_dot(p, v_ref[0, pl.ds(koff, nkeys), :]) / l for p, l, koff in zip(probs, sums, koffs)]
        for r, o in zip(rws, outs):
            o_ref[0, pl.ds(pl.multiple_of(r * w, w), w), :] = o.astype(o_ref.dtype)
        return carry

    lax.fori_loop(0, rows // NAT_ROW_GROUP, body, 0)


def _natten(qkv, bias_rows, batch, seq):
    heads, hd = qkv.shape[0] // 3, qkv.shape[2]
    rows = seq // GRID_W
    assert hd == NAT_HEAD_DIM and qkv.shape[1] == batch * seq
    assert rows >= WIN_H and seq % GRID_W == 0 and rows % NAT_ROW_GROUP == 0
    est = 8 * seq * hd * 2 + 2 * N_BIAS_PAIRS * LANES * 4 + N_BIAS_PAIRS * GRID_W * LANES * 4 + (4 << 20)
    return pl.pallas_call(
        functools.partial(_nat_kernel, rows=rows),
        grid=(batch, heads),
        in_specs=[pl.BlockSpec((1, seq, hd), lambda b, h: (h, b, 0)),
                  pl.BlockSpec((1, seq, hd), lambda b, h: (heads + h, b, 0)),
                  pl.BlockSpec((1, seq, hd), lambda b, h: (2 * heads + h, b, 0)),
                  pl.BlockSpec((1, N_BIAS_PAIRS, LANES), lambda b, h: (h, 0, 0))],
        out_specs=pl.BlockSpec((1, seq, hd), lambda b, h: (h, b, 0)),
        out_shape=jax.ShapeDtypeStruct((heads, batch * seq, hd), BF16),
        scratch_shapes=[pltpu.VMEM((N_BIAS_PAIRS, GRID_W, 2 * GRID_W), F32)],
        compiler_params=_params(("parallel", "arbitrary"), est),
        name="natten",
    )(qkv, qkv, qkv, bias_rows)


def _log_sigmoid(z):
    return jnp.minimum(z, 0.0) - jnp.log(1.0 + jnp.exp(-jnp.abs(z)))


def _bf16_terms(x, n):
    out = []
    for _ in range(n - 1):
        t = x.astype(BF16)
        out.append(t)
        x = x - t.astype(F32)
    out.append(x.astype(BF16))
    return out


def _gla_masks(reverse):
    cs, c = GLA_SUPER, GLA_CHUNK
    ri = lax.broadcasted_iota(jnp.int32, (cs, cs), 0)
    ci = lax.broadcasted_iota(jnp.int32, (cs, cs), 1)
    tri = (ci >= ri) if reverse else (ci <= ri)
    same = (ri & -c) == (ci & -c)
    mblk = jnp.where(same & tri, 1.0, 0.0).astype(BF16)
    r2 = lax.broadcasted_iota(jnp.int32, (2 * c, 2 * c), 0)
    c2 = lax.broadcasted_iota(jnp.int32, (2 * c, 2 * c), 1)
    keep2 = (c2 >= r2) if reverse else (c2 <= r2)
    return mblk, keep2


def _gla_super(q, k, v, ga, w_hi, w_lo, bg, st_ref, masks, *, reverse):
    cs, c = GLA_SUPER, GLA_CHUNK
    nb = cs // c
    assert nb == 4
    dk = q.shape[-1]
    mblk, keep2 = masks

    a_hi, a_lo = _bf16_terms(ga, 2)
    z = _dot(a_hi, w_hi) + _dot(a_hi, w_lo) + _dot(a_lo, w_hi) + bg
    g = _log_sigmoid(z) * (1.0 / GLA_GATE_NORM)
    yield
    gl = sum(_dot(mblk, t) for t in _bf16_terms(g, 3))
    yield

    def rows(x, b):
        return x[b * c:(b + 1) * c]

    tot = [gl[b * c:b * c + 1] if reverse else gl[(b + 1) * c - 1:(b + 1) * c] for b in range(nb)]
    tb = jnp.concatenate([jnp.broadcast_to(t, (c, dk)) for t in tot], axis=0)
    dloc = gl - tb
    qd = q.astype(F32) * jnp.exp(dloc)
    kd = k.astype(F32) * jnp.exp(-dloc)
    yield

    order = list(range(nb - 1, -1, -1)) if reverse else list(range(nb))
    ex = jnp.exp

    def cat_tok(blocks, axis=0):
        return jnp.concatenate([blocks[b] for b in sorted(blocks)], axis=axis)

    halves = [(order[0], order[1]), (order[2], order[3])]
    a_half, k_late = [], []
    for b0, b1 in halves:
        kc0 = cat_tok({b0: rows(kd, b0), b1: rows(kd, b1)}).astype(BF16)
        kc1 = cat_tok({b0: rows(kd, b0) * ex(tot[b1]), b1: rows(kd, b1)}).astype(BF16)
        s0 = _dot_nt(rows(qd, b0).astype(BF16), kc0)
        s1 = _dot_nt(rows(qd, b1).astype(BF16), kc1)
        a = cat_tok({b0: s0, b1: s1})
        a_half.append(jnp.where(keep2, a, 0.0).astype(BF16))
        k_late.append(kc1)
        yield
    (b0, b1), (b2, b3) = halves
    qx = cat_tok({b2: rows(qd, b2) * ex(tot[b2]), b3: rows(qd, b3) * ex(tot[b3] + tot[b2])}).astype(BF16)
    a_cross = _dot_nt(qx, k_late[0]).astype(BF16)
    yield

    lo0, lo1 = min(b0, b1) * c, min(b2, b3) * c
    v0, v1 = v[lo0:lo0 + 2 * c], v[lo1:lo1 + 2 * c]
    o0 = _dot(a_half[0], v0)
    o1 = _dot(a_half[1], v1) + _dot(a_cross, v0)
    yield

    before, after = {}, {}
    run = None
    for b in order:
        before[b] = run
        run = tot[b] if run is None else run + tot[b]
    total = run
    run = None
    for b in reversed(order):
        after[b] = run
        run = tot[b] if run is None else run + tot[b]
    qs = cat_tok({b: rows(qd, b) * ex(tot[b] if before[b] is None else tot[b] + before[b]) for b in order})
    ku = cat_tok({b: rows(kd, b) if after[b] is None else rows(kd, b) * ex(after[b]) for b in order})
    st = st_ref[...]
    o_inter = _dot_nt(qs.astype(BF16), st.astype(BF16))
    yield
    st_ref[...] = st * ex(total) + _dot_tn(v, ku.astype(BF16))
    o_intra = jnp.concatenate([o0, o1] if lo0 < lo1 else [o1, o0], axis=0)
    return o_intra + o_inter


def _run_interleaved(gens):
    results = [None] * len(gens)
    live = list(enumerate(gens))
    while live:
        still = []
        for idx, gen in live:
            try:
                next(gen)
                still.append((idx, gen))
            except StopIteration as stop:
                results[idx] = stop.value
        live = still
    return results


def _gla_heads(q_ref, k_ref, v_ref, ga_ref, wbh_ref, wbl_ref, bg_ref, state_sc, *, reverse):
    nh = state_sc.shape[0]
    dvh, dkh = state_sc.shape[1], state_sc.shape[2]

    @pl.when(pl.program_id(2) == 0)
    def _():
        state_sc[...] = jnp.zeros_like(state_sc)

    masks = _gla_masks(reverse)
    ga = ga_ref[0]
    gens = []
    for u in range(nh):
        ksl = slice(u * dkh, (u + 1) * dkh)
        vsl = slice(u * dvh, (u + 1) * dvh)
        gens.append(_gla_super(q_ref[0, :, ksl], k_ref[0, :, ksl], v_ref[0, :, vsl], ga,
                               wbh_ref[:, ksl], wbl_ref[:, ksl], bg_ref[:, ksl], state_sc.at[u], masks,
                               reverse=reverse))
    return _run_interleaved(gens)


def _gla_dir_kernel(q_ref, k_ref, v_ref, ga_ref, wbh_ref, wbl_ref, bg_ref, o_ref, state_sc, *, reverse):
    dvh = state_sc.shape[1]
    outs = _gla_heads(q_ref, k_ref, v_ref, ga_ref, wbh_ref, wbl_ref, bg_ref, state_sc, reverse=reverse)
    for u, o in enumerate(outs):
        o_ref[0, :, u * dvh:(u + 1) * dvh] = o


def _gla_final_kernel(q_ref, k_ref, v_ref, ga_ref, wbh_ref, wbl_ref, bg_ref, ob_ref, r_ref, gain_ref,
                      o_ref, state_sc):
    dvh = state_sc.shape[1]
    outs = _gla_heads(q_ref, k_ref, v_ref, ga_ref, wbh_ref, wbl_ref, bg_ref, state_sc, reverse=False)
    for u, o in enumerate(outs):
        vsl = slice(u * dvh, (u + 1) * dvh)
        o = o + ob_ref[0, :, vsl]
        o = o * lax.rsqrt(jnp.mean(o * o, axis=-1, keepdims=True) + EPS) * gain_ref[...]
        r = r_ref[0, :, vsl].astype(F32)
        o_ref[0, :, vsl] = (o * (r * jax.nn.sigmoid(r))).astype(o_ref.dtype)


def _gla_mix(proj, ga, w_gate_b, b_gate, gn_gain, batch, seq):
    dk = w_gate_b.shape[-1]
    dv = (proj.shape[1] - 2 * dk) // 2
    dkh, dvh = dk // GLA_HEADS, dv // GLA_HEADS
    nh = GLA_HEADS_PER_STEP
    hk, hv = nh * dkh, nh * dvh
    cs = GLA_SUPER
    assert seq % cs == 0 and GLA_HEADS % nh == 0
    n = seq // cs
    rk = GLA_RANK
    proj3 = proj.reshape(batch, seq, proj.shape[1])
    ga3 = ga.reshape(batch, seq, 2 * rk)
    zeros = jnp.zeros((rk, dk), F32)
    wb = (jnp.concatenate([w_gate_b[0].astype(F32), zeros], axis=0),
          jnp.concatenate([zeros, w_gate_b[1].astype(F32)], axis=0))
    wb_hi = [w.astype(BF16) for w in wb]
    wb_lo = [(w - h.astype(F32)).astype(BF16) for w, h in zip(wb, wb_hi)]
    bg = b_gate.astype(F32).reshape(2, 1, dk)
    gain = gn_gain.astype(F32).reshape(1, dvh)
    kq, kv = dk // hk, (2 * dk) // hv

    def specs(cidx):
        return [pl.BlockSpec((1, cs, hk), lambda b, h, i: (b, cidx(i), h)),
                pl.BlockSpec((1, cs, hk), lambda b, h, i: (b, cidx(i), kq + h)),
                pl.BlockSpec((1, cs, hv), lambda b, h, i: (b, cidx(i), kv + h)),
                pl.BlockSpec((1, cs, 2 * rk), lambda b, h, i: (b, cidx(i), 0)),
                pl.BlockSpec((2 * rk, hk), lambda b, h, i: (0, h)),
                pl.BlockSpec((2 * rk, hk), lambda b, h, i: (0, h)),
                pl.BlockSpec((1, hk), lambda b, h, i: (0, h))]

    est = nh * (dvh * dkh * 4 * 3 + 8 * cs * (2 * dkh + 3 * dvh) * 4 + 24 * cs * dkh * 4) + (4 << 20)
    sem = ("parallel", "parallel", "arbitrary")
    grid = (batch, GLA_HEADS // nh, n)
    state = [pltpu.VMEM((nh, dvh, dkh), F32)]
    rev = lambda i: n - 1 - i
    fwd = lambda i: i
    o_b = pl.pallas_call(
        functools.partial(_gla_dir_kernel, reverse=True),
        grid=grid,
        in_specs=specs(rev),
        out_specs=pl.BlockSpec((1, cs, hv), lambda b, h, i: (b, rev(i), h)),
        out_shape=jax.ShapeDtypeStruct((batch, seq, dv), F32),
        scratch_shapes=state,
        compiler_params=_params(sem, est),
        name="gla_backward",
    )(proj3, proj3, proj3, ga3, wb_hi[1], wb_lo[1], bg[1])
    out = pl.pallas_call(
        _gla_final_kernel,
        grid=grid,
        in_specs=specs(fwd) + [
            pl.BlockSpec((1, cs, hv), lambda b, h, i: (b, i, h)),
            pl.BlockSpec((1, cs, hv), lambda b, h, i: (b, i, kv + GLA_HEADS // nh + h)),
            pl.BlockSpec((1, dvh), lambda b, h, i: (0, 0))],
        out_specs=pl.BlockSpec((1, cs, hv), lambda b, h, i: (b, i, h)),
        out_shape=jax.ShapeDtypeStruct((batch, seq, dv), BF16),
        scratch_shapes=state,
        compiler_params=_params(sem, est),
        name="gla_forward_final",
    )(proj3, proj3, proj3, ga3, wb_hi[0], wb_lo[0], bg[0], o_b, proj3, gain)
    return out.reshape(batch * seq, dv)


def _trunk(x, wts):
    batch, seq, d = x.shape
    x = x.reshape(batch * seq, d)
    depth = wts["norm_mix"].shape[0]
    xg, ssq = _prep_norm(x, wts["norm_mix"][0])
    for i in range(depth):
        j = i // 2
        if i % 2 == 0:
            qkv = _mm_norm(xg, ssq, wts["nat_w_qkv"], j, BF16, scaled_cols=d, scale=NAT_HEAD_DIM ** -0.5,
                           head_major=True)
            mix = _natten(qkv, wts["nat_bias_rows"][j], batch, seq)
            w_o = wts["nat_w_o"]
        else:
            dk = d // 2
            proj = _mm_norm(xg, ssq, wts["gla_w_main"], j, BF16, scaled_cols=dk, scale=(dk // GLA_HEADS) ** -0.5)
            ga = _mm_norm(xg, ssq, wts["gla_w_ga"], j, F32)
            mix = _gla_mix(proj, ga, wts["gla_w_gate_b"][j], wts["gla_b_gate"][j], wts["gla_norm"][j], batch, seq)
            w_o = wts["gla_w_o"]
        x, xg, ssq = _mm_res(mix, w_o, j, x, wts["norm_ffn"][i])
        act = _swiglu_up(xg, ssq, wts["ffn_w_gu"], i)
        if i + 1 < depth:
            x, xg, ssq = _mm_res_wres(act, wts["ffn_w_down"], i, x, wts["norm_mix"][i + 1])
        else:
            x = _mm_res_wres(act, wts["ffn_w_down"], i, x, None)
    y = _rmsnorm(x, wts["final_norm"], F32)
    return y.reshape(batch, seq, d)


def kernel(x_prompt, x_sample, norm_mix, norm_ffn, final_norm, nat_w_qkv, nat_rpb, nat_w_o,
           gla_w_in, gla_w_gate_b, gla_b_gate, gla_norm, gla_w_o, ffn_w_gu, ffn_w_down):
    d = x_prompt.shape[-1]
    n_main = 2 * (d // 2) + 2 * d
    wts = {
        "norm_mix": norm_mix, "norm_ffn": norm_ffn, "final_norm": final_norm,
        "nat_w_qkv": nat_w_qkv.astype(BF16),
        "nat_bias_rows": jax.vmap(_bias_rows)(nat_rpb.astype(F32)),
        "nat_w_o": nat_w_o.astype(BF16),
        "gla_w_main": gla_w_in[:, :, :n_main].astype(BF16),
        "gla_w_ga": gla_w_in[:, :, n_main:].astype(BF16),
        "gla_w_gate_b": gla_w_gate_b, "gla_b_gate": gla_b_gate, "gla_norm": gla_norm,
        "gla_w_o": gla_w_o.astype(BF16),
        "ffn_w_gu": ffn_w_gu.astype(BF16),
        "ffn_w_down": ffn_w_down.astype(BF16),
    }
    return (_trunk(x_prompt, wts), _trunk(x_sample, wts))
```

```python
import functools

import jax
import jax.numpy as jnp
from jax import lax
from jax.experimental import pallas as pl
from jax.experimental.pallas import tpu as pltpu

GRID_W = 64
EPS = 1e-6
NAT_HEAD_DIM = 128
WIN_H = 8
WIN_W = 16
GLA_HEADS = 4
GLA_RANK = 16
GLA_GATE_NORM = 16.0
GLA_CHUNK = 64
GLA_SUPER = 256
GLA_HEADS_PER_STEP_BWD = 4
GLA_HEADS_PER_STEP_FWD = 4

V7X_VMEM_REQUEST_CAP = 58 * 1024 * 1024
LANES = 128

NEG_BIG = -1e30
F32 = jnp.float32
BF16 = jnp.bfloat16


def _params(sem, est_bytes):
    limit = min(max(int(est_bytes * 1.25) + (4 << 20), 16 << 20), V7X_VMEM_REQUEST_CAP)
    return pltpu.CompilerParams(dimension_semantics=sem, vmem_limit_bytes=limit)


def _pick(n, prefs):
    for p in prefs:
        if n % p == 0:
            return p
    return n


def _dot(a, b):
    return jnp.dot(a, b, preferred_element_type=F32)


def _dot_nt(a, b):
    return lax.dot_general(a, b, (((1,), (1,)), ((), ())), preferred_element_type=F32)


def _dot_tn(a, b):
    return lax.dot_general(a, b, (((0,), (0,)), ((), ())), preferred_element_type=F32)


def _lane_partial_sumsq(x):
    x2 = x * x
    acc = x2[:, :LANES]
    for t in range(1, x.shape[1] // LANES):
        acc = acc + x2[:, t * LANES:(t + 1) * LANES]
    return acc


def _rstd(ssq_ref, d):
    s = ssq_ref[0]
    for p in range(1, ssq_ref.shape[0]):
        s = s + ssq_ref[p]
    return lax.rsqrt(jnp.sum(s, axis=-1, keepdims=True) * (1.0 / d) + EPS)


def _prep_kernel(x_ref, g_ref, xg_ref, ssq_ref):
    x = x_ref[...]
    xg_ref[...] = (x * g_ref[...]).astype(xg_ref.dtype)
    ssq_ref[...] = _lane_partial_sumsq(x)


def _prep_norm(x, g):
    m, d = x.shape
    tr = _pick(m, (256, 128, 64, 8))
    est = 4 * tr * d * 4 + 2 * tr * d * 2
    return pl.pallas_call(
        _prep_kernel,
        grid=(m // tr,),
        in_specs=[pl.BlockSpec((tr, d), lambda i: (i, 0)),
                  pl.BlockSpec((1, d), lambda i: (0, 0))],
        out_specs=[pl.BlockSpec((tr, d), lambda i: (i, 0)),
                   pl.BlockSpec((None, tr, LANES), lambda i: (0, i, 0))],
        out_shape=[jax.ShapeDtypeStruct((m, d), BF16),
                   jax.ShapeDtypeStruct((1, m, LANES), F32)],
        compiler_params=_params(("parallel",), est),
        name="norm_prep",
    )(x, g.reshape(1, d).astype(F32))


def _rmsnorm_kernel(x_ref, g_ref, o_ref):
    x = x_ref[...]
    ms = jnp.mean(x * x, axis=-1, keepdims=True)
    o_ref[...] = (x * lax.rsqrt(ms + EPS) * g_ref[...]).astype(o_ref.dtype)


def _rmsnorm(x, g, out_dtype):
    m, d = x.shape
    tr = _pick(m, (256, 128, 64, 8))
    est = 2 * tr * d * 4 + 2 * tr * d * jnp.dtype(out_dtype).itemsize + 2 * tr * d * 4
    return pl.pallas_call(
        _rmsnorm_kernel,
        grid=(m // tr,),
        in_specs=[pl.BlockSpec((tr, d), lambda i: (i, 0)),
                  pl.BlockSpec((1, d), lambda i: (0, 0))],
        out_specs=pl.BlockSpec((tr, d), lambda i: (i, 0)),
        out_shape=jax.ShapeDtypeStruct((m, d), out_dtype),
        compiler_params=_params(("parallel",), est),
        name="rmsnorm",
    )(x, g.reshape(1, d).astype(F32))


def _mm_norm_kernel(a_ref, ssq_ref, w_ref, o_ref, *, scaled_blocks, scale, d, head_major):
    acc = _dot(a_ref[...], w_ref[...]) * _rstd(ssq_ref, d)
    if scaled_blocks:
        acc = acc * jnp.where(pl.program_id(1) < scaled_blocks, scale, 1.0).astype(F32)
    if head_major:
        for h in range(o_ref.shape[0]):
            o_ref[h] = acc[:, h * LANES:(h + 1) * LANES].astype(o_ref.dtype)
    else:
        o_ref[...] = acc.astype(o_ref.dtype)


def _mm_norm(xg, ssq, w, layer, out_dtype, *, scaled_cols=0, scale=1.0, head_major=False):
    m, k = xg.shape
    n = w.shape[2]
    tm = _pick(m, (1024, 512, 256, 128))
    tn = _pick(n, tuple(p for p in (1024, 512, 256, 128) if scaled_cols % p == 0))
    ob = jnp.dtype(out_dtype).itemsize
    ps = ssq.shape[0]
    est = 2 * tm * k * 2 + 2 * k * tn * 2 + 2 * tm * tn * ob + 2 * tm * tn * 4 + 2 * ps * tm * LANES * 4
    if head_major:
        out_spec = pl.BlockSpec((tn // LANES, tm, LANES), lambda i, j: (j, i, 0))
        out_shape = jax.ShapeDtypeStruct((n // LANES, m, LANES), out_dtype)
    else:
        out_spec = pl.BlockSpec((tm, tn), lambda i, j: (i, j))
        out_shape = jax.ShapeDtypeStruct((m, n), out_dtype)
    return pl.pallas_call(
        functools.partial(_mm_norm_kernel, scaled_blocks=scaled_cols // tn, scale=scale, d=k,
                          head_major=head_major),
        grid=(m // tm, n // tn),
        in_specs=[pl.BlockSpec((tm, k), lambda i, j: (i, 0)),
                  pl.BlockSpec((ps, tm, LANES), lambda i, j: (0, i, 0)),
                  pl.BlockSpec((None, k, tn), lambda i, j: (layer, 0, j))],
        out_specs=out_spec,
        out_shape=out_shape,
        compiler_params=_params(("parallel", "arbitrary"), est),
        name="matmul_norm",
    )(xg, ssq, w)


def _res_epilogue(x, g_ref, o_ref, xg_ref):
    o_ref[...] = x
    xg_ref[...] = (x * g_ref[...]).astype(xg_ref.dtype)


def _mm_res_kernel(a_ref, w_ref, r_ref, g_ref, o_ref, xg_ref, ssq_ref, *a_scratch):
    if a_scratch:
        a_sc, = a_scratch

        @pl.when(pl.program_id(1) == 0)
        def _():
            for h in range(a_ref.shape[0]):
                a_sc[:, h * LANES:(h + 1) * LANES] = a_ref[h]

        a = a_sc[...]
    else:
        a = a_ref[...]
    x = r_ref[...] + _dot(a, w_ref[...])
    _res_epilogue(x, g_ref, o_ref, xg_ref)

    @pl.when(pl.program_id(1) == 0)
    def _():
        ssq_ref[...] = jnp.zeros_like(ssq_ref)

    ssq_ref[...] += _lane_partial_sumsq(x)


def _mm_res(a, w, layer, res, gain):
    grouped = a.ndim == 3
    m = a.shape[1] if grouped else a.shape[0]
    k, n = w.shape[1], w.shape[2]
    tm = _pick(m, (1024, 512, 256, 128))
    tn = _pick(n, (512, 256, 128))
    est = (3 if grouped else 2) * tm * k * 2 + 2 * k * tn * 2 + 6 * tm * tn * 4 + 2 * tm * tn * 2 + 2 * tm * LANES * 4
    a_spec = (pl.BlockSpec((k // LANES, tm, LANES), lambda i, j: (0, i, 0)) if grouped
              else pl.BlockSpec((tm, k), lambda i, j: (i, 0)))
    tile = lambda: pl.BlockSpec((tm, tn), lambda i, j: (i, j))
    return pl.pallas_call(
        _mm_res_kernel,
        grid=(m // tm, n // tn),
        in_specs=[a_spec,
                  pl.BlockSpec((None, k, tn), lambda i, j: (layer, 0, j)),
                  tile(),
                  pl.BlockSpec((1, tn), lambda i, j: (0, j))],
        out_specs=[tile(), tile(), pl.BlockSpec((None, tm, LANES), lambda i, j: (0, i, 0))],
        out_shape=[jax.ShapeDtypeStruct((m, n), F32),
                   jax.ShapeDtypeStruct((m, n), BF16),
                   jax.ShapeDtypeStruct((1, m, LANES), F32)],
        scratch_shapes=[pltpu.VMEM((tm, k), BF16)] if grouped else [],
        compiler_params=_params(("parallel", "arbitrary"), est),
        name="matmul_residual_norm",
    )(a, w, res, gain.reshape(1, n).astype(F32))


def _mm_res_wres_kernel(a_ref, w_ref, r_ref, g_ref, o_ref, xg_ref, ssq_ref):
    x = r_ref[...] + _dot(a_ref[...], w_ref[...])
    _res_epilogue(x, g_ref, o_ref, xg_ref)
    ssq_ref[...] = _lane_partial_sumsq(x)


def _mm_res_wres_plain_kernel(a_ref, w_ref, r_ref, o_ref):
    o_ref[...] = r_ref[...] + _dot(a_ref[...], w_ref[...])


def _mm_res_wres(a, w, layer, res, gain):
    m, k = a.shape
    n = w.shape[2]
    tm = _pick(m, (256, 128))
    tn = _pick(n, (1024, 512, 256, 128))
    est = 2 * tm * k * 2 + k * tn * 2 + 6 * tm * tn * 4 + 2 * tm * tn * 2 + 2 * tm * LANES * 4
    tile = lambda: pl.BlockSpec((tm, tn), lambda j, i: (i, j))
    in_specs = [pl.BlockSpec((tm, k), lambda j, i: (i, 0)),
                pl.BlockSpec((None, k, tn), lambda j, i: (layer, 0, j), pipeline_mode=pl.Buffered(1)),
                tile()]
    x_shape = jax.ShapeDtypeStruct((m, n), F32)
    cp = _params(("arbitrary", "arbitrary"), est)
    grid = (n // tn, m // tm)
    if gain is None:
        return pl.pallas_call(
            _mm_res_wres_plain_kernel, grid=grid, in_specs=in_specs, out_specs=tile(),
            out_shape=x_shape, compiler_params=cp, name="matmul_residual_wres",
        )(a, w, res)
    return pl.pallas_call(
        _mm_res_wres_kernel,
        grid=grid,
        in_specs=in_specs + [pl.BlockSpec((1, tn), lambda j, i: (0, j))],
        out_specs=[tile(), tile(), pl.BlockSpec((None, tm, LANES), lambda j, i: (j, i, 0))],
        out_shape=[x_shape,
                   jax.ShapeDtypeStruct((m, n), BF16),
                   jax.ShapeDtypeStruct((n // tn, m, LANES), F32)],
        compiler_params=cp,
        name="matmul_residual_wres_norm",
    )(a, w, res, gain.reshape(1, n).astype(F32))


def _swiglu_kernel(a_ref, ssq_ref, wg_ref, wu_ref, o_ref, *, d):
    a = a_ref[...]
    rstd = _rstd(ssq_ref, d)
    g = _dot(a, wg_ref[...]) * rstd
    u = _dot(a, wu_ref[...]) * rstd
    o_ref[...] = (g * jax.nn.sigmoid(g) * u).astype(o_ref.dtype)


def _swiglu_up(xg, ssq, w_gu, layer):
    m, k = xg.shape
    dff = w_gu.shape[2] // 2
    tm = _pick(m, (2048, 1024, 512, 256, 128))
    tn = _pick(dff, (256, 128))
    nb = dff // tn
    ps = ssq.shape[0]
    est = 2 * tm * k * 2 + 4 * k * tn * 2 + 2 * tm * tn * 2 + 4 * tm * tn * 4 + 2 * ps * tm * LANES * 4
    return pl.pallas_call(
        functools.partial(_swiglu_kernel, d=k),
        grid=(m // tm, nb),
        in_specs=[pl.BlockSpec((tm, k), lambda i, j: (i, 0)),
                  pl.BlockSpec((ps, tm, LANES), lambda i, j: (0, i, 0)),
                  pl.BlockSpec((None, k, tn), lambda i, j: (layer, 0, j)),
                  pl.BlockSpec((None, k, tn), lambda i, j: (layer, 0, nb + j))],
        out_specs=pl.BlockSpec((tm, tn), lambda i, j: (i, j)),
        out_shape=jax.ShapeDtypeStruct((m, dff), BF16),
        compiler_params=_params(("parallel", "arbitrary"), est),
        name="swiglu_up",
    )(xg, ssq, w_gu, w_gu)


N_BIAS_PAIRS = 2 * WIN_H - 2
NAT_ROW_GROUP = 16


def _bias_rows(rpb):
    h = rpb.shape[0]
    z = jnp.zeros((h, N_BIAS_PAIRS, GRID_W - 2 * WIN_W + 1), F32)
    lo = rpb[:, :N_BIAS_PAIRS]
    hi = rpb[:, 1:N_BIAS_PAIRS + 1]
    return jnp.concatenate([lo[:, :, WIN_W - 1:], z, hi, z, lo[:, :, :WIN_W - 1]], axis=-1).astype(F32)


def _nat_kernel(q_ref, k_ref, v_ref, base_ref, o_ref, bias_sc, *, rows):
    w = GRID_W
    qc = lax.broadcasted_iota(jnp.int32, (w, 2 * w), 0)
    kc = lax.broadcasted_iota(jnp.int32, (w, 2 * w), 1) & (w - 1)
    cs = jnp.clip(qc - WIN_W // 2, 0, w - WIN_W)
    valid = (kc >= cs) & (kc < cs + WIN_W)
    for p in range(N_BIAS_PAIRS):
        b = jnp.broadcast_to(base_ref[0, p:p + 1, :], (w, 2 * w))
        t = pltpu.roll(b, 0, 1, stride=1, stride_axis=0)
        bias_sc[p] = jnp.where(valid, t, NEG_BIG)

    kh = WIN_H
    nkeys = kh * w

    def body(it, carry):
        rws = [it * NAT_ROW_GROUP + u for u in range(NAT_ROW_GROUP)]
        starts = [jnp.clip(r - kh // 2, 0, rows - kh) for r in rws]
        koffs = [pl.multiple_of(rs * w, w) for rs in starts]
        scores = []
        for r, rs, koff in zip(rws, starts, koffs):
            q = q_ref[0, pl.ds(pl.multiple_of(r * w, w), w), :]
            s = _dot_nt(q, k_ref[0, pl.ds(koff, nkeys), :])
            var = r - rs
            bias = jnp.concatenate([bias_sc[2 * g + (WIN_H - 1) - var] for g in range(kh // 2)], axis=1)
            scores.append(s + bias)
        probs, sums = [], []
        for s in scores:
            p = jnp.exp(s - jnp.max(s, axis=-1, keepdims=True))
            sums.append(jnp.sum(p, axis=-1, keepdims=True))
            probs.append(p.astype(BF16))
        outs = [_dot(p, v_ref[0, pl.ds(koff, nkeys), :]) / l for p, l, koff in zip(probs, sums, koffs)]
        for r, o in zip(rws, outs):
            o_ref[0, pl.ds(pl.multiple_of(r * w, w), w), :] = o.astype(o_ref.dtype)
        return carry

    lax.fori_loop(0, rows // NAT_ROW_GROUP, body, 0)


def _natten(qkv, bias_rows, batch, seq):
    heads, hd = qkv.shape[0] // 3, qkv.shape[2]
    rows = seq // GRID_W
    assert hd == NAT_HEAD_DIM and qkv.shape[1] == batch * seq
    assert rows >= WIN_H and seq % GRID_W == 0 and rows % NAT_ROW_GROUP == 0
    est = 8 * seq * hd * 2 + 2 * N_BIAS_PAIRS * LANES * 4 + N_BIAS_PAIRS * GRID_W * LANES * 4 + (4 << 20)
    return pl.pallas_call(
        functools.partial(_nat_kernel, rows=rows),
        grid=(batch, heads),
        in_specs=[pl.BlockSpec((1, seq, hd), lambda b, h: (h, b, 0)),
                  pl.BlockSpec((1, seq, hd), lambda b, h: (heads + h, b, 0)),
                  pl.BlockSpec((1, seq, hd), lambda b, h: (2 * heads + h, b, 0)),
                  pl.BlockSpec((1, N_BIAS_PAIRS, LANES), lambda b, h: (h, 0, 0))],
        out_specs=pl.BlockSpec((1, seq, hd), lambda b, h: (h, b, 0)),
        out_shape=jax.ShapeDtypeStruct((heads, batch * seq, hd), BF16),
        scratch_shapes=[pltpu.VMEM((N_BIAS_PAIRS, GRID_W, 2 * GRID_W), F32)],
        compiler_params=_params(("parallel", "arbitrary"), est),
        name="natten",
    )(qkv, qkv, qkv, bias_rows)


def _log_sigmoid(z):
    return jnp.minimum(z, 0.0) - jnp.log(1.0 + jnp.exp(-jnp.abs(z)))


def _bf16_terms(x, n):
    out = []
    for _ in range(n - 1):
        t = x.astype(BF16)
        out.append(t)
        x = x - t.astype(F32)
    out.append(x.astype(BF16))
    return out


def _gla_masks(reverse):
    cs, c = GLA_SUPER, GLA_CHUNK
    ri = lax.broadcasted_iota(jnp.int32, (cs, cs), 0)
    ci = lax.broadcasted_iota(jnp.int32, (cs, cs), 1)
    tri = (ci >= ri) if reverse else (ci <= ri)
    same = (ri & -c) == (ci & -c)
    mblk = jnp.where(same & tri, 1.0, 0.0).astype(BF16)
    r2 = lax.broadcasted_iota(jnp.int32, (2 * c, 2 * c), 0)
    c2 = lax.broadcasted_iota(jnp.int32, (2 * c, 2 * c), 1)
    keep2 = (c2 >= r2) if reverse else (c2 <= r2)
    return mblk, keep2


def _gla_super(q, k, v, ga, w_hi, w_lo, bg, st_ref, masks, *, reverse):
    cs, c = GLA_SUPER, GLA_CHUNK
    nb = cs // c
    assert nb == 4
    dk = q.shape[-1]
    mblk, keep2 = masks

    a_hi, a_lo = _bf16_terms(ga, 2)
    z = _dot(a_hi, w_hi) + _dot(a_hi, w_lo) + _dot(a_lo, w_hi) + bg
    g = _log_sigmoid(z) * (1.0 / GLA_GATE_NORM)
    yield
    gl = sum(_dot(mblk, t) for t in _bf16_terms(g, 3))
    yield

    def rows(x, b):
        return x[b * c:(b + 1) * c]

    tot = [gl[b * c:b * c + 1] if reverse else gl[(b + 1) * c - 1:(b + 1) * c] for b in range(nb)]
    tb = jnp.concatenate([jnp.broadcast_to(t, (c, dk)) for t in tot], axis=0)
    dloc = gl - tb
    qd = q.astype(F32) * jnp.exp(dloc)
    kd = k.astype(F32) * jnp.exp(-dloc)
    yield

    order = list(range(nb - 1, -1, -1)) if reverse else list(range(nb))
    ex = jnp.exp

    def cat_tok(blocks, axis=0):
        return jnp.concatenate([blocks[b] for b in sorted(blocks)], axis=axis)

    halves = [(order[0], order[1]), (order[2], order[3])]
    a_half, k_late = [], []
    for b0, b1 in halves:
        kc0 = cat_tok({b0: rows(kd, b0), b1: rows(kd, b1)}).astype(BF16)
        kc1 = cat_tok({b0: rows(kd, b0) * ex(tot[b1]), b1: rows(kd, b1)}).astype(BF16)
        s0 = _dot_nt(rows(qd, b0).astype(BF16), kc0)
        s1 = _dot_nt(rows(qd, b1).astype(BF16), kc1)
        a = cat_tok({b0: s0, b1: s1})
        a_half.append(jnp.where(keep2, a, 0.0).astype(BF16))
        k_late.append(kc1)
        yield
    (b0, b1), (b2, b3) = halves
    qx = cat_tok({b2: rows(qd, b2) * ex(tot[b2]), b3: rows(qd, b3) * ex(tot[b3] + tot[b2])}).astype(BF16)
    a_cross = _dot_nt(qx, k_late[0]).astype(BF16)
    yield

    lo0, lo1 = min(b0, b1) * c, min(b2, b3) * c
    v0, v1 = v[lo0:lo0 + 2 * c], v[lo1:lo1 + 2 * c]
    o0 = _dot(a_half[0], v0)
    o1 = _dot(a_half[1], v1) + _dot(a_cross, v0)
    yield

    before, after = {}, {}
    run = None
    for b in order:
        before[b] = run
        run = tot[b] if run is None else run + tot[b]
    total = run
    run = None
    for b in reversed(order):
        after[b] = run
        run = tot[b] if run is None else run + tot[b]
    qs = cat_tok({b: rows(qd, b) * ex(tot[b] if before[b] is None else tot[b] + before[b]) for b in order})
    ku = cat_tok({b: rows(kd, b) if after[b] is None else rows(kd, b) * ex(after[b]) for b in order})
    st = st_ref[...]
    o_inter = _dot_nt(qs.astype(BF16), st.astype(BF16))
    yield
    st_ref[...] = st * ex(total) + _dot_tn(v, ku.astype(BF16))
    o_intra = jnp.concatenate([o0, o1] if lo0 < lo1 else [o1, o0], axis=0)
    return o_intra + o_inter


def _run_interleaved(gens):
    results = [None] * len(gens)
    live = list(enumerate(gens))
    while live:
        still = []
        for idx, gen in live:
            try:
                next(gen)
                still.append((idx, gen))
            except StopIteration as stop:
                results[idx] = stop.value
        live = still
    return results


def _gla_heads(q_ref, k_ref, v_ref, ga_ref, wbh_ref, wbl_ref, bg_ref, state_sc, *, reverse):
    nh = state_sc.shape[0]
    dvh, dkh = state_sc.shape[1], state_sc.shape[2]

    @pl.when(pl.program_id(2) == 0)
    def _():
        state_sc[...] = jnp.zeros_like(state_sc)

    masks = _gla_masks(reverse)
    ga = ga_ref[0]
    gens = []
    for u in range(nh):
        ksl = slice(u * dkh, (u + 1) * dkh)
        vsl = slice(u * dvh, (u + 1) * dvh)
        gens.append(_gla_super(q_ref[0, :, ksl], k_ref[0, :, ksl], v_ref[0, :, vsl], ga,
                               wbh_ref[:, ksl], wbl_ref[:, ksl], bg_ref[:, ksl], state_sc.at[u], masks,
                               reverse=reverse))
    return _run_interleaved(gens)


def _gla_dir_kernel(q_ref, k_ref, v_ref, ga_ref, wbh_ref, wbl_ref, bg_ref, o_ref, state_sc, *, reverse):
    dvh = state_sc.shape[1]
    outs = _gla_heads(q_ref, k_ref, v_ref, ga_ref, wbh_ref, wbl_ref, bg_ref, state_sc, reverse=reverse)
    for u, o in enumerate(outs):
        o_ref[0, :, u * dvh:(u + 1) * dvh] = o


def _gla_final_kernel(q_ref, k_ref, v_ref, ga_ref, wbh_ref, wbl_ref, bg_ref, ob_ref, r_ref, gain_ref,
                      o_ref, state_sc):
    dvh = state_sc.shape[1]
    outs = _gla_heads(q_ref, k_ref, v_ref, ga_ref, wbh_ref, wbl_ref, bg_ref, state_sc, reverse=False)
    for u, o in enumerate(outs):
        vsl = slice(u * dvh, (u + 1) * dvh)
        o = o + ob_ref[0, :, vsl]
        o = o * lax.rsqrt(jnp.mean(o * o, axis=-1, keepdims=True) + EPS) * gain_ref[...]
        r = r_ref[0, :, vsl].astype(F32)
        o_ref[0, :, vsl] = (o * (r * jax.nn.sigmoid(r))).astype(o_ref.dtype)


def _gla_mix(proj, ga, w_gate_b, b_gate, gn_gain, batch, seq):
    dk = w_gate_b.shape[-1]
    dv = (proj.shape[1] - 2 * dk) // 2
    dkh, dvh = dk // GLA_HEADS, dv // GLA_HEADS
    cs = GLA_SUPER
    assert seq % cs == 0
    n = seq // cs
    rk = GLA_RANK
    proj3 = proj.reshape(batch, seq, proj.shape[1])
    ga3 = ga.reshape(batch, seq, 2 * rk)
    zeros = jnp.zeros((rk, dk), F32)
    wb = (jnp.concatenate([w_gate_b[0].astype(F32), zeros], axis=0),
          jnp.concatenate([zeros, w_gate_b[1].astype(F32)], axis=0))
    wb_hi = [w.astype(BF16) for w in wb]
    wb_lo = [(w - h.astype(F32)).astype(BF16) for w, h in zip(wb, wb_hi)]
    bg = b_gate.astype(F32).reshape(2, 1, dk)
    gain = gn_gain.astype(F32).reshape(1, dvh)
    sem = ("parallel", "parallel", "arbitrary")

    def run(kernel_fn, name, nh, cidx, direction, extra_inputs, extra_specs, out_dtype):
        assert GLA_HEADS % nh == 0
        hk, hv = nh * dkh, nh * dvh
        kq, kv = dk // hk, (2 * dk) // hv
        est = nh * (dvh * dkh * 4 * 3 + 8 * cs * (2 * dkh + 3 * dvh) * 4 + 24 * cs * dkh * 4) + (4 << 20)
        return pl.pallas_call(
            kernel_fn,
            grid=(batch, GLA_HEADS // nh, n),
            in_specs=[pl.BlockSpec((1, cs, hk), lambda b, h, i: (b, cidx(i), h)),
                      pl.BlockSpec((1, cs, hk), lambda b, h, i: (b, cidx(i), kq + h)),
                      pl.BlockSpec((1, cs, hv), lambda b, h, i: (b, cidx(i), kv + h)),
                      pl.BlockSpec((1, cs, 2 * rk), lambda b, h, i: (b, cidx(i), 0)),
                      pl.BlockSpec((2 * rk, hk), lambda b, h, i: (0, h)),
                      pl.BlockSpec((2 * rk, hk), lambda b, h, i: (0, h)),
                      pl.BlockSpec((1, hk), lambda b, h, i: (0, h))] + extra_specs(hv, kv + GLA_HEADS // nh),
            out_specs=pl.BlockSpec((1, cs, hv), lambda b, h, i: (b, cidx(i), h)),
            out_shape=jax.ShapeDtypeStruct((batch, seq, dv), out_dtype),
            scratch_shapes=[pltpu.VMEM((nh, dvh, dkh), F32)],
            compiler_params=_params(sem, est),
            name=name,
        )(proj3, proj3, proj3, ga3, wb_hi[direction], wb_lo[direction], bg[direction], *extra_inputs)

    o_b = run(functools.partial(_gla_dir_kernel, reverse=True), "gla_backward", GLA_HEADS_PER_STEP_BWD,
              lambda i: n - 1 - i, 1, (), lambda hv, r0: [], F32)
    out = run(_gla_final_kernel, "gla_forward_final", GLA_HEADS_PER_STEP_FWD, lambda i: i, 0,
              (o_b, proj3, gain),
              lambda hv, r0: [pl.BlockSpec((1, cs, hv), lambda b, h, i: (b, i, h)),
                              pl.BlockSpec((1, cs, hv), lambda b, h, i: (b, i, r0 + h)),
                              pl.BlockSpec((1, dvh), lambda b, h, i: (0, 0))],
              BF16)
    return out.reshape(batch * seq, dv)


def _trunk(x, wts):
    batch, seq, d = x.shape
    x = x.reshape(batch * seq, d)
    depth = wts["norm_mix"].shape[0]
    xg, ssq = _prep_norm(x, wts["norm_mix"][0])
    for i in range(depth):
        j = i // 2
        if i % 2 == 0:
            qkv = _mm_norm(xg, ssq, wts["nat_w_qkv"], j, BF16, scaled_cols=d, scale=NAT_HEAD_DIM ** -0.5,
                           head_major=True)
            mix = _natten(qkv, wts["nat_bias_rows"][j], batch, seq)
            w_o = wts["nat_w_o"]
        else:
            dk = d // 2
            proj = _mm_norm(xg, ssq, wts["gla_w_main"], j, BF16, scaled_cols=dk, scale=(dk // GLA_HEADS) ** -0.5)
            ga = _mm_norm(xg, ssq, wts["gla_w_ga"], j, F32)
            mix = _gla_mix(proj, ga, wts["gla_w_gate_b"][j], wts["gla_b_gate"][j], wts["gla_norm"][j], batch, seq)
            w_o = wts["gla_w_o"]
        x, xg, ssq = _mm_res(mix, w_o, j, x, wts["norm_ffn"][i])
        act = _swiglu_up(xg, ssq, wts["ffn_w_gu"], i)
        if i + 1 < depth:
            x, xg, ssq = _mm_res_wres(act, wts["ffn_w_down"], i, x, wts["norm_mix"][i + 1])
        else:
            x = _mm_res_wres(act, wts["ffn_w_down"], i, x, None)
    y = _rmsnorm(x, wts["final_norm"], F32)
    return y.reshape(batch, seq, d)


def kernel(x_prompt, x_sample, norm_mix, norm_ffn, final_norm, nat_w_qkv, nat_rpb, nat_w_o,
           gla_w_in, gla_w_gate_b, gla_b_gate, gla_norm, gla_w_o, ffn_w_gu, ffn_w_down):
    d = x_prompt.shape[-1]
    n_main = 2 * (d // 2) + 2 * d
    wts = {
        "norm_mix": norm_mix, "norm_ffn": norm_ffn, "final_norm": final_norm,
        "nat_w_qkv": nat_w_qkv.astype(BF16),
        "nat_bias_rows": jax.vmap(_bias_rows)(nat_rpb.astype(F32)),
        "nat_w_o": nat_w_o.astype(BF16),
        "gla_w_main": gla_w_in[:, :, :n_main].astype(BF16),
        "gla_w_ga": gla_w_in[:, :, n_main:].astype(BF16),
        "gla_w_gate_b": gla_w_gate_b, "gla_b_gate": gla_b_gate, "gla_norm": gla_norm,
        "gla_w_o": gla_w_o.astype(BF16),
        "ffn_w_gu": ffn_w_gu.astype(BF16),
        "ffn_w_down": ffn_w_down.astype(BF16),
    }
    return (_trunk(x_prompt, wts), _trunk(x_sample, wts))
```

```python
import functools

import jax
import jax.numpy as jnp
from jax import lax
from jax.experimental import pallas as pl
from jax.experimental.pallas import tpu as pltpu

GRID_W = 64
EPS = 1e-6
NAT_HEAD_DIM = 128
WIN_H = 8
WIN_W = 16
GLA_HEADS = 4
GLA_RANK = 16
GLA_GATE_NORM = 16.0
GLA_CHUNK = 64
GLA_SUPER = 256
GLA_HEADS_PER_STEP_BWD = 4
GLA_HEADS_PER_STEP_FWD = 4

V7X_VMEM_REQUEST_CAP = 58 * 1024 * 1024
LANES = 128

NEG_BIG = -1e30
F32 = jnp.float32
BF16 = jnp.bfloat16


def _params(sem, est_bytes):
    limit = min(max(int(est_bytes * 1.25) + (4 << 20), 16 << 20), V7X_VMEM_REQUEST_CAP)
    return pltpu.CompilerParams(dimension_semantics=sem, vmem_limit_bytes=limit)


def _pick(n, prefs):
    for p in prefs:
        if n % p == 0:
            return p
    return n


def _dot(a, b):
    return jnp.dot(a, b, preferred_element_type=F32)


def _dot_nt(a, b):
    return lax.dot_general(a, b, (((1,), (1,)), ((), ())), preferred_element_type=F32)


def _dot_tn(a, b):
    return lax.dot_general(a, b, (((0,), (0,)), ((), ())), preferred_element_type=F32)


def _lane_partial_sumsq(x):
    x2 = x * x
    acc = x2[:, :LANES]
    for t in range(1, x.shape[1] // LANES):
        acc = acc + x2[:, t * LANES:(t + 1) * LANES]
    return acc


def _rstd(ssq_ref, d):
    s = ssq_ref[0]
    for p in range(1, ssq_ref.shape[0]):
        s = s + ssq_ref[p]
    return lax.rsqrt(jnp.sum(s, axis=-1, keepdims=True) * (1.0 / d) + EPS)


def _prep_kernel(x_ref, g_ref, xg_ref, ssq_ref):
    x = x_ref[...]
    xg_ref[...] = (x * g_ref[...]).astype(xg_ref.dtype)
    ssq_ref[...] = _lane_partial_sumsq(x)


def _prep_norm(x, g):
    m, d = x.shape
    tr = _pick(m, (256, 128, 64, 8))
    est = 4 * tr * d * 4 + 2 * tr * d * 2
    return pl.pallas_call(
        _prep_kernel,
        grid=(m // tr,),
        in_specs=[pl.BlockSpec((tr, d), lambda i: (i, 0)),
                  pl.BlockSpec((1, d), lambda i: (0, 0))],
        out_specs=[pl.BlockSpec((tr, d), lambda i: (i, 0)),
                   pl.BlockSpec((None, tr, LANES), lambda i: (0, i, 0))],
        out_shape=[jax.ShapeDtypeStruct((m, d), BF16),
                   jax.ShapeDtypeStruct((1, m, LANES), F32)],
        compiler_params=_params(("parallel",), est),
        name="norm_prep",
    )(x, g.reshape(1, d).astype(F32))


def _rmsnorm_kernel(x_ref, g_ref, o_ref):
    x = x_ref[...]
    ms = jnp.mean(x * x, axis=-1, keepdims=True)
    o_ref[...] = (x * lax.rsqrt(ms + EPS) * g_ref[...]).astype(o_ref.dtype)


def _rmsnorm(x, g, out_dtype):
    m, d = x.shape
    tr = _pick(m, (256, 128, 64, 8))
    est = 2 * tr * d * 4 + 2 * tr * d * jnp.dtype(out_dtype).itemsize + 2 * tr * d * 4
    return pl.pallas_call(
        _rmsnorm_kernel,
        grid=(m // tr,),
        in_specs=[pl.BlockSpec((tr, d), lambda i: (i, 0)),
                  pl.BlockSpec((1, d), lambda i: (0, 0))],
        out_specs=pl.BlockSpec((tr, d), lambda i: (i, 0)),
        out_shape=jax.ShapeDtypeStruct((m, d), out_dtype),
        compiler_params=_params(("parallel",), est),
        name="rmsnorm",
    )(x, g.reshape(1, d).astype(F32))


def _mm_norm_kernel(a_ref, ssq_ref, w_ref, o_ref, *, scaled_blocks, scale, d, head_major):
    acc = _dot(a_ref[...], w_ref[...]) * _rstd(ssq_ref, d)
    if scaled_blocks:
        acc = acc * jnp.where(pl.program_id(1) < scaled_blocks, scale, 1.0).astype(F32)
    if head_major:
        for h in range(o_ref.shape[0]):
            o_ref[h] = acc[:, h * LANES:(h + 1) * LANES].astype(o_ref.dtype)
    else:
        o_ref[...] = acc.astype(o_ref.dtype)


def _mm_norm(xg, ssq, w, layer, out_dtype, *, n_cols=None, scaled_cols=0, scale=1.0, head_major=False):
    m, k = xg.shape
    n = w.shape[2] if n_cols is None else n_cols
    tm = _pick(m, (1024, 512, 256, 128))
    tn = _pick(n, tuple(p for p in (1024, 512, 256, 128) if scaled_cols % p == 0))
    assert n % tn == 0
    ob = jnp.dtype(out_dtype).itemsize
    ps = ssq.shape[0]
    est = 2 * tm * k * 2 + 2 * k * tn * 2 + 2 * tm * tn * ob + 2 * tm * tn * 4 + 2 * ps * tm * LANES * 4
    if head_major:
        out_spec = pl.BlockSpec((tn // LANES, tm, LANES), lambda i, j: (j, i, 0))
        out_shape = jax.ShapeDtypeStruct((n // LANES, m, LANES), out_dtype)
    else:
        out_spec = pl.BlockSpec((tm, tn), lambda i, j: (i, j))
        out_shape = jax.ShapeDtypeStruct((m, n), out_dtype)
    return pl.pallas_call(
        functools.partial(_mm_norm_kernel, scaled_blocks=scaled_cols // tn, scale=scale, d=k,
                          head_major=head_major),
        grid=(m // tm, n // tn),
        in_specs=[pl.BlockSpec((tm, k), lambda i, j: (i, 0)),
                  pl.BlockSpec((ps, tm, LANES), lambda i, j: (0, i, 0)),
                  pl.BlockSpec((None, k, tn), lambda i, j: (layer, 0, j))],
        out_specs=out_spec,
        out_shape=out_shape,
        compiler_params=_params(("parallel", "arbitrary"), est),
        name="matmul_norm",
    )(xg, ssq, w)


def _res_epilogue(x, g_ref, o_ref, xg_ref):
    o_ref[...] = x
    xg_ref[...] = (x * g_ref[...]).astype(xg_ref.dtype)


def _mm_res_kernel(a_ref, w_ref, r_ref, g_ref, o_ref, xg_ref, ssq_ref, *a_scratch):
    if a_scratch:
        a_sc, = a_scratch

        @pl.when(pl.program_id(1) == 0)
        def _():
            for h in range(a_ref.shape[0]):
                a_sc[:, h * LANES:(h + 1) * LANES] = a_ref[h]

        a = a_sc[...]
    else:
        a = a_ref[...]
    x = r_ref[...] + _dot(a, w_ref[...])
    _res_epilogue(x, g_ref, o_ref, xg_ref)

    @pl.when(pl.program_id(1) == 0)
    def _():
        ssq_ref[...] = jnp.zeros_like(ssq_ref)

    ssq_ref[...] += _lane_partial_sumsq(x)


def _mm_res(a, w, layer, res, gain):
    grouped = a.ndim == 3
    m = a.shape[1] if grouped else a.shape[0]
    k, n = w.shape[1], w.shape[2]
    tm = _pick(m, (1024, 512, 256, 128))
    tn = _pick(n, (512, 256, 128))
    est = (3 if grouped else 2) * tm * k * 2 + 2 * k * tn * 2 + 6 * tm * tn * 4 + 2 * tm * tn * 2 + 2 * tm * LANES * 4
    a_spec = (pl.BlockSpec((k // LANES, tm, LANES), lambda i, j: (0, i, 0)) if grouped
              else pl.BlockSpec((tm, k), lambda i, j: (i, 0)))
    tile = lambda: pl.BlockSpec((tm, tn), lambda i, j: (i, j))
    return pl.pallas_call(
        _mm_res_kernel,
        grid=(m // tm, n // tn),
        in_specs=[a_spec,
                  pl.BlockSpec((None, k, tn), lambda i, j: (layer, 0, j)),
                  tile(),
                  pl.BlockSpec((1, tn), lambda i, j: (0, j))],
        out_specs=[tile(), tile(), pl.BlockSpec((None, tm, LANES), lambda i, j: (0, i, 0))],
        out_shape=[jax.ShapeDtypeStruct((m, n), F32),
                   jax.ShapeDtypeStruct((m, n), BF16),
                   jax.ShapeDtypeStruct((1, m, LANES), F32)],
        scratch_shapes=[pltpu.VMEM((tm, k), BF16)] if grouped else [],
        compiler_params=_params(("parallel", "arbitrary"), est),
        name="matmul_residual_norm",
    )(a, w, res, gain.reshape(1, n).astype(F32))


def _mm_res_wres_kernel(a_ref, w_ref, r_ref, g_ref, o_ref, xg_ref, ssq_ref):
    x = r_ref[...] + _dot(a_ref[...], w_ref[...])
    _res_epilogue(x, g_ref, o_ref, xg_ref)
    ssq_ref[...] = _lane_partial_sumsq(x)


def _mm_res_wres_plain_kernel(a_ref, w_ref, r_ref, o_ref):
    o_ref[...] = r_ref[...] + _dot(a_ref[...], w_ref[...])


def _mm_res_wres(a, w, layer, res, gain):
    m, k = a.shape
    n = w.shape[2]
    tm = _pick(m, (256, 128))
    tn = _pick(n, (1024, 512, 256, 128))
    est = 2 * tm * k * 2 + k * tn * 2 + 6 * tm * tn * 4 + 2 * tm * tn * 2 + 2 * tm * LANES * 4
    tile = lambda: pl.BlockSpec((tm, tn), lambda j, i: (i, j))
    in_specs = [pl.BlockSpec((tm, k), lambda j, i: (i, 0)),
                pl.BlockSpec((None, k, tn), lambda j, i: (layer, 0, j), pipeline_mode=pl.Buffered(1)),
                tile()]
    x_shape = jax.ShapeDtypeStruct((m, n), F32)
    cp = _params(("arbitrary", "arbitrary"), est)
    grid = (n // tn, m // tm)
    if gain is None:
        return pl.pallas_call(
            _mm_res_wres_plain_kernel, grid=grid, in_specs=in_specs, out_specs=tile(),
            out_shape=x_shape, compiler_params=cp, name="matmul_residual_wres",
        )(a, w, res)
    return pl.pallas_call(
        _mm_res_wres_kernel,
        grid=grid,
        in_specs=in_specs + [pl.BlockSpec((1, tn), lambda j, i: (0, j))],
        out_specs=[tile(), tile(), pl.BlockSpec((None, tm, LANES), lambda j, i: (j, i, 0))],
        out_shape=[x_shape,
                   jax.ShapeDtypeStruct((m, n), BF16),
                   jax.ShapeDtypeStruct((n // tn, m, LANES), F32)],
        compiler_params=cp,
        name="matmul_residual_wres_norm",
    )(a, w, res, gain.reshape(1, n).astype(F32))


def _swiglu_kernel(a_ref, ssq_ref, wg_ref, wu_ref, o_ref, *, d):
    a = a_ref[...]
    rstd = _rstd(ssq_ref, d)
    g = _dot(a, wg_ref[...]) * rstd
    u = _dot(a, wu_ref[...]) * rstd
    o_ref[...] = (g * jax.nn.sigmoid(g) * u).astype(o_ref.dtype)


def _swiglu_up(xg, ssq, w_gu, layer):
    m, k = xg.shape
    dff = w_gu.shape[2] // 2
    tm = _pick(m, (2048, 1024, 512, 256, 128))
    tn = _pick(dff, (256, 128))
    nb = dff // tn
    ps = ssq.shape[0]
    est = 2 * tm * k * 2 + 4 * k * tn * 2 + 2 * tm * tn * 2 + 4 * tm * tn * 4 + 2 * ps * tm * LANES * 4
    return pl.pallas_call(
        functools.partial(_swiglu_kernel, d=k),
        grid=(m // tm, nb),
        in_specs=[pl.BlockSpec((tm, k), lambda i, j: (i, 0)),
                  pl.BlockSpec((ps, tm, LANES), lambda i, j: (0, i, 0)),
                  pl.BlockSpec((None, k, tn), lambda i, j: (layer, 0, j)),
                  pl.BlockSpec((None, k, tn), lambda i, j: (layer, 0, nb + j))],
        out_specs=pl.BlockSpec((tm, tn), lambda i, j: (i, j)),
        out_shape=jax.ShapeDtypeStruct((m, dff), BF16),
        compiler_params=_params(("parallel", "arbitrary"), est),
        name="swiglu_up",
    )(xg, ssq, w_gu, w_gu)


N_BIAS_PAIRS = 2 * WIN_H - 2
NAT_ROW_GROUP = 32


def _bias_rows(rpb):
    h = rpb.shape[0]
    z = jnp.zeros((h, N_BIAS_PAIRS, GRID_W - 2 * WIN_W + 1), F32)
    lo = rpb[:, :N_BIAS_PAIRS]
    hi = rpb[:, 1:N_BIAS_PAIRS + 1]
    return jnp.concatenate([lo[:, :, WIN_W - 1:], z, hi, z, lo[:, :, :WIN_W - 1]], axis=-1).astype(F32)


def _nat_kernel(q_ref, k_ref, v_ref, base_ref, o_ref, bias_sc, *, rows):
    w = GRID_W
    qc = lax.broadcasted_iota(jnp.int32, (w, 2 * w), 0)
    kc = lax.broadcasted_iota(jnp.int32, (w, 2 * w), 1) & (w - 1)
    cs = jnp.clip(qc - WIN_W // 2, 0, w - WIN_W)
    valid = (kc >= cs) & (kc < cs + WIN_W)
    for p in range(N_BIAS_PAIRS):
        b = jnp.broadcast_to(base_ref[0, p:p + 1, :], (w, 2 * w))
        t = pltpu.roll(b, 0, 1, stride=1, stride_axis=0)
        bias_sc[p] = jnp.where(valid, t, NEG_BIG)

    kh = WIN_H
    nkeys = kh * w

    def body(it, carry):
        rws = [it * NAT_ROW_GROUP + u for u in range(NAT_ROW_GROUP)]
        starts = [jnp.clip(r - kh // 2, 0, rows - kh) for r in rws]
        koffs = [pl.multiple_of(rs * w, w) for rs in starts]
        scores = []
        for r, rs, koff in zip(rws, starts, koffs):
            q = q_ref[0, pl.ds(pl.multiple_of(r * w, w), w), :]
            s = _dot_nt(q, k_ref[0, pl.ds(koff, nkeys), :])
            var = r - rs
            bias = jnp.concatenate([bias_sc[2 * g + (WIN_H - 1) - var] for g in range(kh // 2)], axis=1)
            scores.append(s + bias)
        probs, sums = [], []
        for s in scores:
            p = jnp.exp(s - jnp.max(s, axis=-1, keepdims=True))
            sums.append(jnp.sum(p, axis=-1, keepdims=True))
            probs.append(p.astype(BF16))
        outs = [_dot(p, v_ref[0, pl.ds(koff, nkeys), :]) / l for p, l, koff in zip(probs, sums, koffs)]
        for r, o in zip(rws, outs):
            o_ref[0, pl.ds(pl.multiple_of(r * w, w), w), :] = o.astype(o_ref.dtype)
        return carry

    lax.fori_loop(0, rows // NAT_ROW_GROUP, body, 0)


def _natten(qkv, bias_rows, batch, seq):
    heads, hd = qkv.shape[0] // 3, qkv.shape[2]
    rows = seq // GRID_W
    assert hd == NAT_HEAD_DIM and qkv.shape[1] == batch * seq
    assert rows >= WIN_H and seq % GRID_W == 0 and rows % NAT_ROW_GROUP == 0
    est = 8 * seq * hd * 2 + 2 * N_BIAS_PAIRS * LANES * 4 + N_BIAS_PAIRS * GRID_W * LANES * 4 + (4 << 20)
    return pl.pallas_call(
        functools.partial(_nat_kernel, rows=rows),
        grid=(batch, heads),
        in_specs=[pl.BlockSpec((1, seq, hd), lambda b, h: (h, b, 0)),
                  pl.BlockSpec((1, seq, hd), lambda b, h: (heads + h, b, 0)),
                  pl.BlockSpec((1, seq, hd), lambda b, h: (2 * heads + h, b, 0)),
                  pl.BlockSpec((1, N_BIAS_PAIRS, LANES), lambda b, h: (h, 0, 0))],
        out_specs=pl.BlockSpec((1, seq, hd), lambda b, h: (h, b, 0)),
        out_shape=jax.ShapeDtypeStruct((heads, batch * seq, hd), BF16),
        scratch_shapes=[pltpu.VMEM((N_BIAS_PAIRS, GRID_W, 2 * GRID_W), F32)],
        compiler_params=_params(("parallel", "arbitrary"), est),
        name="natten",
    )(qkv, qkv, qkv, bias_rows)


def _log_sigmoid(z):
    return jnp.minimum(z, 0.0) - jnp.log(1.0 + jnp.exp(-jnp.abs(z)))


def _bf16_terms(x, n):
    out = []
    for _ in range(n - 1):
        t = x.astype(BF16)
        out.append(t)
        x = x - t.astype(F32)
    out.append(x.astype(BF16))
    return out


def _gla_masks(reverse):
    cs, c = GLA_SUPER, GLA_CHUNK
    ri = lax.broadcasted_iota(jnp.int32, (cs, cs), 0)
    ci = lax.broadcasted_iota(jnp.int32, (cs, cs), 1)
    tri = (ci >= ri) if reverse else (ci <= ri)
    same = (ri & -c) == (ci & -c)
    mblk = jnp.where(same & tri, 1.0, 0.0).astype(BF16)
    r2 = lax.broadcasted_iota(jnp.int32, (2 * c, 2 * c), 0)
    c2 = lax.broadcasted_iota(jnp.int32, (2 * c, 2 * c), 1)
    keep2 = (c2 >= r2) if reverse else (c2 <= r2)
    return mblk, keep2


def _gla_super(q, k, v, ga, w_hi, w_lo, bg, st_ref, masks, *, reverse):
    cs, c = GLA_SUPER, GLA_CHUNK
    nb = cs // c
    assert nb == 4
    dk = q.shape[-1]
    mblk, keep2 = masks

    a_hi, a_lo = _bf16_terms(ga, 2)
    z = _dot(a_hi, w_hi) + _dot(a_hi, w_lo) + _dot(a_lo, w_hi) + bg
    g = _log_sigmoid(z) * (1.0 / GLA_GATE_NORM)
    yield
    gl = sum(_dot(mblk, t) for t in _bf16_terms(g, 3))
    yield

    def rows(x, b):
        return x[b * c:(b + 1) * c]

    tot = [gl[b * c:b * c + 1] if reverse else gl[(b + 1) * c - 1:(b + 1) * c] for b in range(nb)]
    tb = jnp.concatenate([jnp.broadcast_to(t, (c, dk)) for t in tot], axis=0)
    dloc = gl - tb
    qd = q.astype(F32) * jnp.exp(dloc)
    kd = k.astype(F32) * jnp.exp(-dloc)
    yield

    order = list(range(nb - 1, -1, -1)) if reverse else list(range(nb))
    ex = jnp.exp

    def cat_tok(blocks, axis=0):
        return jnp.concatenate([blocks[b] for b in sorted(blocks)], axis=axis)

    halves = [(order[0], order[1]), (order[2], order[3])]
    a_half, k_late = [], []
    for b0, b1 in halves:
        kc0 = cat_tok({b0: rows(kd, b0), b1: rows(kd, b1)}).astype(BF16)
        kc1 = cat_tok({b0: rows(kd, b0) * ex(tot[b1]), b1: rows(kd, b1)}).astype(BF16)
        s0 = _dot_nt(rows(qd, b0).astype(BF16), kc0)
        s1 = _dot_nt(rows(qd, b1).astype(BF16), kc1)
        a = cat_tok({b0: s0, b1: s1})
        a_half.append(jnp.where(keep2, a, 0.0).astype(BF16))
        k_late.append(kc1)
        yield
    (b0, b1), (b2, b3) = halves
    qx = cat_tok({b2: rows(qd, b2) * ex(tot[b2]), b3: rows(qd, b3) * ex(tot[b3] + tot[b2])}).astype(BF16)
    a_cross = _dot_nt(qx, k_late[0]).astype(BF16)
    yield

    lo0, lo1 = min(b0, b1) * c, min(b2, b3) * c
    v0, v1 = v[lo0:lo0 + 2 * c], v[lo1:lo1 + 2 * c]
    o0 = _dot(a_half[0], v0)
    o1 = _dot(a_half[1], v1) + _dot(a_cross, v0)
    yield

    before, after = {}, {}
    run = None
    for b in order:
        before[b] = run
        run = tot[b] if run is None else run + tot[b]
    total = run
    run = None
    for b in reversed(order):
        after[b] = run
        run = tot[b] if run is None else run + tot[b]
    qs = cat_tok({b: rows(qd, b) * ex(tot[b] if before[b] is None else tot[b] + before[b]) for b in order})
    ku = cat_tok({b: rows(kd, b) if after[b] is None else rows(kd, b) * ex(after[b]) for b in order})
    st = st_ref[...]
    o_inter = _dot_nt(qs.astype(BF16), st.astype(BF16))
    yield
    st_ref[...] = st * ex(total) + _dot_tn(v, ku.astype(BF16))
    o_intra = jnp.concatenate([o0, o1] if lo0 < lo1 else [o1, o0], axis=0)
    return o_intra + o_inter


def _run_interleaved(gens):
    results = [None] * len(gens)
    live = list(enumerate(gens))
    while live:
        still = []
        for idx, gen in live:
            try:
                next(gen)
                still.append((idx, gen))
            except StopIteration as stop:
                results[idx] = stop.value
        live = still
    return results


def _gla_heads(q_ref, k_ref, v_ref, ga_ref, wbh_ref, wbl_ref, bg_ref, state_sc, *, reverse):
    nh = state_sc.shape[0]
    dvh, dkh = state_sc.shape[1], state_sc.shape[2]

    @pl.when(pl.program_id(2) == 0)
    def _():
        state_sc[...] = jnp.zeros_like(state_sc)

    masks = _gla_masks(reverse)
    ga = ga_ref[0]
    gens = []
    for u in range(nh):
        ksl = slice(u * dkh, (u + 1) * dkh)
        vsl = slice(u * dvh, (u + 1) * dvh)
        gens.append(_gla_super(q_ref[0, :, ksl], k_ref[0, :, ksl], v_ref[0, :, vsl], ga,
                               wbh_ref[:, ksl], wbl_ref[:, ksl], bg_ref[:, ksl], state_sc.at[u], masks,
                               reverse=reverse))
    return _run_interleaved(gens)


def _gla_dir_kernel(q_ref, k_ref, v_ref, ga_ref, wbh_ref, wbl_ref, bg_ref, o_ref, state_sc, *, reverse):
    dvh = state_sc.shape[1]
    outs = _gla_heads(q_ref, k_ref, v_ref, ga_ref, wbh_ref, wbl_ref, bg_ref, state_sc, reverse=reverse)
    for u, o in enumerate(outs):
        o_ref[0, :, u * dvh:(u + 1) * dvh] = o


def _gla_final_kernel(q_ref, k_ref, v_ref, ga_ref, wbh_ref, wbl_ref, bg_ref, ob_ref, r_ref, gain_ref,
                      o_ref, state_sc):
    dvh = state_sc.shape[1]
    outs = _gla_heads(q_ref, k_ref, v_ref, ga_ref, wbh_ref, wbl_ref, bg_ref, state_sc, reverse=False)
    for u, o in enumerate(outs):
        vsl = slice(u * dvh, (u + 1) * dvh)
        o = o + ob_ref[0, :, vsl]
        o = o * lax.rsqrt(jnp.mean(o * o, axis=-1, keepdims=True) + EPS) * gain_ref[...]
        r = r_ref[0, :, vsl].astype(F32)
        o_ref[0, :, vsl] = (o * (r * jax.nn.sigmoid(r))).astype(o_ref.dtype)


def _gla_mix(proj, ga, w_gate_b, b_gate, gn_gain, batch, seq):
    dk = w_gate_b.shape[-1]
    dv = (proj.shape[1] - 2 * dk) // 2
    dkh, dvh = dk // GLA_HEADS, dv // GLA_HEADS
    cs = GLA_SUPER
    assert seq % cs == 0
    n = seq // cs
    rk = GLA_RANK
    proj3 = proj.reshape(batch, seq, proj.shape[1])
    ga3 = ga.reshape(batch, seq, 2 * rk)
    zeros = jnp.zeros((rk, dk), F32)
    wb = (jnp.concatenate([w_gate_b[0].astype(F32), zeros], axis=0),
          jnp.concatenate([zeros, w_gate_b[1].astype(F32)], axis=0))
    wb_hi = [w.astype(BF16) for w in wb]
    wb_lo = [(w - h.astype(F32)).astype(BF16) for w, h in zip(wb, wb_hi)]
    bg = b_gate.astype(F32).reshape(2, 1, dk)
    gain = gn_gain.astype(F32).reshape(1, dvh)
    sem = ("parallel", "parallel", "arbitrary")

    def run(kernel_fn, name, nh, cidx, direction, extra_inputs, extra_specs, out_dtype):
        assert GLA_HEADS % nh == 0
        hk, hv = nh * dkh, nh * dvh
        kq, kv = dk // hk, (2 * dk) // hv
        est = nh * (dvh * dkh * 4 * 3 + 8 * cs * (2 * dkh + 3 * dvh) * 4 + 24 * cs * dkh * 4) + (4 << 20)
        return pl.pallas_call(
            kernel_fn,
            grid=(batch, GLA_HEADS // nh, n),
            in_specs=[pl.BlockSpec((1, cs, hk), lambda b, h, i: (b, cidx(i), h)),
                      pl.BlockSpec((1, cs, hk), lambda b, h, i: (b, cidx(i), kq + h)),
                      pl.BlockSpec((1, cs, hv), lambda b, h, i: (b, cidx(i), kv + h)),
                      pl.BlockSpec((1, cs, 2 * rk), lambda b, h, i: (b, cidx(i), 0)),
                      pl.BlockSpec((2 * rk, hk), lambda b, h, i: (0, h)),
                      pl.BlockSpec((2 * rk, hk), lambda b, h, i: (0, h)),
                      pl.BlockSpec((1, hk), lambda b, h, i: (0, h))] + extra_specs(hv, kv + GLA_HEADS // nh),
            out_specs=pl.BlockSpec((1, cs, hv), lambda b, h, i: (b, cidx(i), h)),
            out_shape=jax.ShapeDtypeStruct((batch, seq, dv), out_dtype),
            scratch_shapes=[pltpu.VMEM((nh, dvh, dkh), F32)],
            compiler_params=_params(sem, est),
            name=name,
        )(proj3, proj3, proj3, ga3, wb_hi[direction], wb_lo[direction], bg[direction], *extra_inputs)

    o_b = run(functools.partial(_gla_dir_kernel, reverse=True), "gla_backward", GLA_HEADS_PER_STEP_BWD,
              lambda i: n - 1 - i, 1, (), lambda hv, r0: [], F32)
    out = run(_gla_final_kernel, "gla_forward_final", GLA_HEADS_PER_STEP_FWD, lambda i: i, 0,
              (o_b, proj3, gain),
              lambda hv, r0: [pl.BlockSpec((1, cs, hv), lambda b, h, i: (b, i, h)),
                              pl.BlockSpec((1, cs, hv), lambda b, h, i: (b, i, r0 + h)),
                              pl.BlockSpec((1, dvh), lambda b, h, i: (0, 0))],
              BF16)
    return out.reshape(batch * seq, dv)


def _trunk(x, wts):
    batch, seq, d = x.shape
    x = x.reshape(batch * seq, d)
    depth = wts["norm_mix"].shape[0]
    xg, ssq = _prep_norm(x, wts["norm_mix"][0])
    for i in range(depth):
        j = i // 2
        if i % 2 == 0:
            qkv = _mm_norm(xg, ssq, wts["nat_w_qkv"], j, BF16, scaled_cols=d, scale=NAT_HEAD_DIM ** -0.5,
                           head_major=True)
            mix = _natten(qkv, wts["nat_bias_rows"][j], batch, seq)
            w_o = wts["nat_w_o"]
        else:
            dk = d // 2
            proj = _mm_norm(xg, ssq, wts["gla_w_in"], j, BF16, n_cols=2 * dk + 2 * d, scaled_cols=dk,
                            scale=(dk // GLA_HEADS) ** -0.5)
            ga = _mm_norm(xg, ssq, wts["gla_w_ga"], j, F32)
            mix = _gla_mix(proj, ga, wts["gla_w_gate_b"][j], wts["gla_b_gate"][j], wts["gla_norm"][j], batch, seq)
            w_o = wts["gla_w_o"]
        x, xg, ssq = _mm_res(mix, w_o, j, x, wts["norm_ffn"][i])
        act = _swiglu_up(xg, ssq, wts["ffn_w_gu"], i)
        if i + 1 < depth:
            x, xg, ssq = _mm_res_wres(act, wts["ffn_w_down"], i, x, wts["norm_mix"][i + 1])
        else:
            x = _mm_res_wres(act, wts["ffn_w_down"], i, x, None)
    y = _rmsnorm(x, wts["final_norm"], F32)
    return y.reshape(batch, seq, d)


def kernel(x_prompt, x_sample, norm_mix, norm_ffn, final_norm, nat_w_qkv, nat_rpb, nat_w_o,
           gla_w_in, gla_w_gate_b, gla_b_gate, gla_norm, gla_w_o, ffn_w_gu, ffn_w_down):
    d = x_prompt.shape[-1]
    n_main = 2 * (d // 2) + 2 * d
    wts = {
        "norm_mix": norm_mix, "norm_ffn": norm_ffn, "final_norm": final_norm,
        "nat_w_qkv": nat_w_qkv.astype(BF16),
        "nat_bias_rows": jax.vmap(_bias_rows)(nat_rpb.astype(F32)),
        "nat_w_o": nat_w_o.astype(BF16),
        "gla_w_in": gla_w_in.astype(BF16),
        "gla_w_ga": gla_w_in[:, :, n_main:].astype(BF16),
        "gla_w_gate_b": gla_w_gate_b, "gla_b_gate": gla_b_gate, "gla_norm": gla_norm,
        "gla_w_o": gla_w_o.astype(BF16),
        "ffn_w_gu": ffn_w_gu.astype(BF16),
        "ffn_w_down": ffn_w_down.astype(BF16),
    }
    return (_trunk(x_prompt, wts), _trunk(x_sample, wts))
```
